```python
import math
import jax, jax.numpy as jnp
from jax import lax
import numpy as np

D_MODEL = 1024
BATCH = 16
SEQ = 4096
DEPTH = 4

GRID_W = 64
CTX_LEN = 256
HEAD_DIM = 64
FOURIER_W = D_MODEL // 2
FOURIER_GROUPS = 4
FOURIER_GROUP_W = FOURIER_W // FOURIER_GROUPS
ATT_HEADS = (D_MODEL - FOURIER_W) // HEAD_DIM
ATT_KV_HEADS = ATT_HEADS // 4
ATT_GROUP = ATT_HEADS // ATT_KV_HEADS
Q_W = ATT_HEADS * HEAD_DIM
KV_W = ATT_KV_HEADS * HEAD_DIM
WINDOW = 128
ATT_BLOCK = 128
ROPE_BASE = 10000.0
EVEN_IN_W = FOURIER_W + Q_W + 2 * KV_W
SSM_W = D_MODEL // 2
SSM_GROUP = 16
SSM_GROUPS = SSM_W // SSM_GROUP
SSM_STATE = 64
GMLP_W = D_MODEL - SSM_W
GMLP_GROUPS = 4
GMLP_GROUP_W = GMLP_W // GMLP_GROUPS
CHUNK = 128
ODD_IN_W = SSM_W + 2 * GMLP_W
MIX_W = D_MODEL
FFN_HIDDEN = D_MODEL * 11 // 4
N_EXPERTS = 8
TOP_K = 2
EXPERT_HIDDEN = D_MODEL * 7 // 2
N_EVEN = (DEPTH + 1) // 2
N_ODD = DEPTH // 2
ALPHA = (2 * DEPTH) ** 0.25
BETA = (8 * DEPTH) ** -0.25
LN_EPS = 1e-5
NEG_INF = -1e30

kernel_name = "hybrid_fourier_swa_s5_gmlp_moe_diffusion_trunk"


def _standardize(x):
    xf = x.astype(jnp.float32)
    mu = jnp.mean(xf, axis=-1, keepdims=True)
    var = jnp.mean(jnp.square(xf - mu), axis=-1, keepdims=True)
    return ((xf - mu) * lax.rsqrt(var + LN_EPS)).astype(x.dtype)


def layer_norm(x, g, b):
    return _standardize(x) * g + b


def _rope_axis(x, pos):
    half = x.shape[-1] // 2
    inv = ROPE_BASE ** (-jnp.arange(half, dtype=jnp.float32) / half)
    ang = pos.astype(jnp.float32)[:, None] * inv[None, :]
    cos, sin = jnp.cos(ang)[:, None, :], jnp.sin(ang)[:, None, :]
    xf = x.astype(jnp.float32)
    x1, x2 = xf[..., :half], xf[..., half:]
    return jnp.concatenate([x1 * cos - x2 * sin, x1 * sin + x2 * cos], axis=-1).astype(x.dtype)


def rope_2d(x, rows, cols):
    d2 = x.shape[-1] // 2
    return jnp.concatenate([_rope_axis(x[..., :d2], rows), _rope_axis(x[..., d2:], cols)], axis=-1)


def fourier_mix(a):
    bsz, n, _ = a.shape
    a4 = a.astype(jnp.float32).reshape(bsz, n, FOURIER_GROUPS, FOURIER_GROUP_W)
    y = jnp.fft.fftn(a4, axes=(1, 3), norm='ortho').real
    return y.reshape(bsz, n, FOURIER_W).astype(a.dtype)


def window_attention(q, k, v, k_ctx, v_ctx, sink):
    bsz, n, _, _ = q.shape
    n_blk = n // ATT_BLOCK
    n_band = 3 * ATT_BLOCK
    n_ctx = k_ctx.shape[1]
    scale = HEAD_DIM ** -0.5
    qg = q.reshape(bsz, n, ATT_KV_HEADS, ATT_GROUP, HEAD_DIM)
    pad = ((0, 0), (ATT_BLOCK, ATT_BLOCK), (0, 0), (0, 0))
    k_pad, v_pad = jnp.pad(k, pad), jnp.pad(v, pad)
    sink_g = sink.astype(jnp.float32).reshape(ATT_KV_HEADS, ATT_GROUP)[None, :, :, None, None]
    offs_q = jnp.arange(ATT_BLOCK)
    offs_k = jnp.arange(n_band) - ATT_BLOCK

    def one_block(blk):
        start = blk * ATT_BLOCK
        qb = lax.dynamic_slice_in_dim(qg, start, ATT_BLOCK, axis=1)
        kb = lax.dynamic_slice_in_dim(k_pad, start, n_band, axis=1)
        vb = lax.dynamic_slice_in_dim(v_pad, start, n_band, axis=1)
        qpos = start + offs_q
        kpos = start + offs_k
        valid = (jnp.abs(qpos[:, None] - kpos[None, :]) <= WINDOW) & (kpos[None, :] >= 0) & (kpos[None, :] < n)
        s_loc = jnp.einsum('bqhgd,bkhd->bhgqk', qb, kb, preferred_element_type=jnp.float32) * scale
        s_loc = jnp.where(valid, s_loc, NEG_INF)
        s_ctx = jnp.einsum('bqhgd,bkhd->bhgqk', qb, k_ctx, preferred_element_type=jnp.float32) * scale
        s_sink = jnp.broadcast_to(sink_g, s_loc.shape[:-1] + (1,))
        p = jax.nn.softmax(jnp.concatenate([s_loc, s_ctx, s_sink], axis=-1), axis=-1)
        p_loc = p[..., :n_band].astype(v.dtype)
        p_ctx = p[..., n_band:n_band + n_ctx].astype(v.dtype)
        return (jnp.einsum('bhgqk,bkhd->bqhgd', p_loc, vb)
                + jnp.einsum('bhgqk,bkhd->bqhgd', p_ctx, v_ctx))

    out = lax.map(one_block, jnp.arange(n_blk))
    return jnp.moveaxis(out, 0, 1).reshape(bsz, n, Q_W)


def context_attention(q, k, v, sink):
    bsz, n_ctx, _, _ = q.shape
    qg = q.reshape(bsz, n_ctx, ATT_KV_HEADS, ATT_GROUP, HEAD_DIM)
    s = jnp.einsum('bqhgd,bkhd->bhgqk', qg, k, preferred_element_type=jnp.float32) * HEAD_DIM ** -0.5
    sink_g = sink.astype(jnp.float32).reshape(ATT_KV_HEADS, ATT_GROUP)[None, :, :, None, None]
    s_sink = jnp.broadcast_to(sink_g, s.shape[:-1] + (1,))
    p = jax.nn.softmax(jnp.concatenate([s, s_sink], axis=-1), axis=-1)[..., :n_ctx].astype(v.dtype)
    return jnp.einsum('bhgqk,bkhd->bqhgd', p, v).reshape(bsz, n_ctx, Q_W)


def even_mixer(h_lat, h_ctx, w_in, w_out, sink, rows, cols, need_ctx):
    bsz, n, _ = h_lat.shape
    n_ctx = h_ctx.shape[1]
    o_q, o_k, o_v = FOURIER_W, FOURIER_W + Q_W, FOURIER_W + Q_W + KV_W
    p_lat = h_lat @ w_in
    p_ctx = h_ctx @ (w_in if need_ctx else w_in[:, o_k:])
    p_kv_ctx = p_ctx[..., o_k:] if need_ctx else p_ctx
    k_ctx = p_kv_ctx[..., :KV_W].reshape(bsz, n_ctx, ATT_KV_HEADS, HEAD_DIM)
    v_ctx = p_kv_ctx[..., KV_W:].reshape(bsz, n_ctx, ATT_KV_HEADS, HEAD_DIM)
    q_lat = rope_2d(p_lat[..., o_q:o_k].reshape(bsz, n, ATT_HEADS, HEAD_DIM), rows, cols)
    k_lat = rope_2d(p_lat[..., o_k:o_v].reshape(bsz, n, ATT_KV_HEADS, HEAD_DIM), rows, cols)
    v_lat = p_lat[..., o_v:].reshape(bsz, n, ATT_KV_HEADS, HEAD_DIM)
    y_lat = jnp.concatenate([fourier_mix(p_lat[..., :o_q]),
                             window_attention(q_lat, k_lat, v_lat, k_ctx, v_ctx, sink)], axis=-1) @ w_out
    y_ctx = None
    if need_ctx:
        q_ctx = p_ctx[..., o_q:o_k].reshape(bsz, n_ctx, ATT_HEADS, HEAD_DIM)
        y_ctx = jnp.concatenate([fourier_mix(p_ctx[..., :o_q]),
                                 context_attention(q_ctx, k_ctx, v_ctx, sink)], axis=-1) @ w_out
    return y_lat, y_ctx


def _zoh(a_re, a_im, log_dt, b_re, b_im):
    lam = lax.complex(a_re.astype(jnp.float32), a_im.astype(jnp.float32))
    dt = jnp.exp(log_dt.astype(jnp.float32))[:, None]
    lam_bar = jnp.exp(lam * dt)
    b_c = lax.complex(b_re.astype(jnp.float32), b_im.astype(jnp.float32))
    return lam_bar, ((lam_bar - 1.0) / lam)[..., None] * b_c


def _lin_combine(e1, e2):
    a1, b1 = e1
    a2, b2 = e2
    return a1 * a2, a2 * b1 + b2


def _diag_scan(lam_bar, bu, reverse):
    a = jnp.broadcast_to(lam_bar, bu.shape)
    _, h = lax.associative_scan(_lin_combine, (a, bu), reverse=reverse, axis=1)
    return h


def bidirectional_s5(u_lat, u_ctx, a_re, a_im, log_dt, b_re, b_im, c_re, c_im, d_skip, need_ctx):
    bsz, n, _ = u_lat.shape
    n_ctx = u_ctx.shape[1]
    ul = u_lat.astype(jnp.float32).reshape(bsz, n, SSM_GROUPS, SSM_GROUP).astype(jnp.complex64)
    uc = u_ctx.astype(jnp.float32).reshape(bsz, n_ctx, SSM_GROUPS, SSM_GROUP).astype(jnp.complex64)
    y_lat = (d_skip * u_lat).astype(jnp.float32)
    y_ctx = (d_skip * u_ctx).astype(jnp.float32) if need_ctx else None
    for direction in range(2):
        reverse = direction == 1
        lam_bar, b_bar = _zoh(a_re[direction], a_im[direction], log_dt[direction],
                              b_re[direction], b_im[direction])
        c_mat = lax.complex(c_re[direction].astype(jnp.float32), c_im[direction].astype(jnp.float32))
        h_ctx = _diag_scan(lam_bar, jnp.einsum('gps,bngs->bngp', b_bar, uc), reverse)
        h_carry = h_ctx[:, 0] if reverse else h_ctx[:, -1]
        bu_lat = jnp.einsum('gps,bngs->bngp', b_bar, ul)
        bu_lat = bu_lat.at[:, -1 if reverse else 0].add(lam_bar * h_carry)
        h_lat = _diag_scan(lam_bar, bu_lat, reverse)
        y_lat = y_lat + jnp.einsum('gsp,bngp->bngs', c_mat, h_lat).real.reshape(bsz, n, SSM_W)
        if need_ctx:
            y_ctx = y_ctx + jnp.einsum('gsp,bngp->bngs', c_mat, h_ctx).real.reshape(bsz, n_ctx, SSM_W)
    return y_lat.astype(u_lat.dtype), (y_ctx.astype(u_ctx.dtype) if need_ctx else None)


def s5_glu(y, w_glu, b_glu):
    g = jax.nn.gelu(y)
    return g * jax.nn.sigmoid(g @ w_glu + b_glu)


def chunk_gmlp(p, w_s, b_s):
    bsz, n, _ = p.shape
    uv = jax.nn.gelu(p)
    u, v = uv[..., :GMLP_W], _standardize(uv[..., GMLP_W:])
    v = v.reshape(bsz, n // CHUNK, CHUNK, GMLP_GROUPS, GMLP_GROUP_W)
    vs = jnp.einsum('hij,bcjhe->bcihe', w_s, v) + jnp.transpose(b_s)[None, None, :, :, None]
    return u * vs.reshape(bsz, n, GMLP_W)


def odd_mixer(h_lat, h_ctx, w_in, a_re, a_im, log_dt, b_re, b_im, c_re, c_im, d_skip,
              w_glu, b_glu, w_s, b_s, w_out, need_ctx):
    p_lat = h_lat @ w_in
    p_ctx = h_ctx @ (w_in if need_ctx else w_in[:, :SSM_W])
    s_lat, s_ctx = bidirectional_s5(p_lat[..., :SSM_W], p_ctx[..., :SSM_W], a_re, a_im, log_dt,
                                    b_re, b_im, c_re, c_im, d_skip, need_ctx)
    y_lat = jnp.concatenate([s5_glu(s_lat, w_glu, b_glu),
                             chunk_gmlp(p_lat[..., SSM_W:], w_s, b_s)], axis=-1) @ w_out
    y_ctx = None
    if need_ctx:
        y_ctx = jnp.concatenate([s5_glu(s_ctx, w_glu, b_glu),
                                 chunk_gmlp(p_ctx[..., SSM_W:], w_s, b_s)], axis=-1) @ w_out
    return y_lat, y_ctx


def dense_swiglu(h, w1, w3, w2):
    return (jax.nn.silu(h @ w1) * (h @ w3)) @ w2


def moe_swiglu(h, w_router, w1, w3, w2):
    logits = (h @ w_router).astype(jnp.float32)
    top_v, top_i = lax.top_k(logits, TOP_K)
    top_w = jax.nn.softmax(top_v, axis=-1)
    gates = jnp.sum(jax.nn.one_hot(top_i, N_EXPERTS, dtype=jnp.float32) * top_w[..., None], axis=-2)
    gates = gates.astype(h.dtype)
    out = jnp.zeros_like(h)
    for e in range(N_EXPERTS):
        he = jax.nn.silu(h @ w1[e]) * (h @ w3[e])
        out = out + gates[..., e:e + 1] * (he @ w2[e])
    return out


def setup_inputs(seed: int = 0) -> dict:
    key = jax.random.key(seed)
    ks = jax.random.split(key, 32)
    f32 = jnp.float32
    d = D_MODEL

    def nrm(i, shape, scale):
        return jax.random.normal(ks[i], shape, f32) * scale

    ssm_shape = (N_ODD, 2, SSM_GROUPS, SSM_STATE)
    a_im_base = math.pi * jnp.arange(SSM_STATE, dtype=f32)
    return {
        'x': nrm(0, (BATCH, SEQ, d), 1.0),
        'c': nrm(1, (BATCH, d), 1.0),
        'ctx': nrm(2, (BATCH, CTX_LEN, d), 1.0),
        'c_ctx': nrm(3, (d,), 1.0),
        'w_mod': nrm(4, (DEPTH, d, 6 * d), d ** -0.5),
        'b_mod': nrm(5, (DEPTH, 6 * d), 0.02),
        'ln_g': 1.0 + nrm(6, (DEPTH, 2, d), 0.05),
        'ln_b': nrm(7, (DEPTH, 2, d), 0.02),
        'ev_w_in': nrm(8, (N_EVEN, d, EVEN_IN_W), d ** -0.5),
        'ev_w_out': nrm(9, (N_EVEN, MIX_W, d), BETA * MIX_W ** -0.5),
        'ev_sink': nrm(10, (N_EVEN, ATT_HEADS), 0.5),
        'ev_w1': nrm(11, (N_EVEN, d, FFN_HIDDEN), d ** -0.5),
        'ev_w3': nrm(12, (N_EVEN, d, FFN_HIDDEN), d ** -0.5),
        'ev_w2': nrm(13, (N_EVEN, FFN_HIDDEN, d), BETA * FFN_HIDDEN ** -0.5),
        'od_w_in': nrm(14, (N_ODD, d, ODD_IN_W), d ** -0.5),
        'ssm_a_re': -0.5 + nrm(15, ssm_shape, 0.01),
        'ssm_a_im': a_im_base + nrm(16, ssm_shape, 0.01),
        'ssm_log_dt': jax.random.uniform(ks[17], (N_ODD, 2, SSM_GROUPS), f32,
                                         math.log(1e-3), math.log(1e-1)),
        'ssm_b_re': nrm(18, (N_ODD, 2, SSM_GROUPS, SSM_STATE, SSM_GROUP), (2 * SSM_GROUP) ** -0.5),
        'ssm_b_im': nrm(19, (N_ODD, 2, SSM_GROUPS, SSM_STATE, SSM_GROUP), (2 * SSM_GROUP) ** -0.5),
        'ssm_c_re': nrm(20, (N_ODD, 2, SSM_GROUPS, SSM_GROUP, SSM_STATE), (2 * SSM_STATE) ** -0.5),
        'ssm_c_im': nrm(21, (N_ODD, 2, SSM_GROUPS, SSM_GROUP, SSM_STATE), (2 * SSM_STATE) ** -0.5),
        'ssm_d': nrm(22, (N_ODD, SSM_W), 0.5),
        'ssm_w_glu': nrm(23, (N_ODD, SSM_W, SSM_W), SSM_W ** -0.5),
        'ssm_b_glu': nrm(24, (N_ODD, SSM_W), 0.02),
        'gmlp_w_s': nrm(25, (N_ODD, GMLP_GROUPS, CHUNK, CHUNK), 0.5 * CHUNK ** -0.5),
        'gmlp_b_s': 1.0 + nrm(26, (N_ODD, GMLP_GROUPS, CHUNK), 0.02),
        'od_w_out': nrm(27, (N_ODD, MIX_W, d), BETA * MIX_W ** -0.5),
        'moe_w_router': nrm(28, (N_ODD, d, N_EXPERTS), d ** -0.5),
        'moe_w1': nrm(29, (N_ODD, N_EXPERTS, d, EXPERT_HIDDEN), d ** -0.5),
        'moe_w3': nrm(30, (N_ODD, N_EXPERTS, d, EXPERT_HIDDEN), d ** -0.5),
        'moe_w2': nrm(31, (N_ODD, N_EXPERTS, EXPERT_HIDDEN, d), BETA * EXPERT_HIDDEN ** -0.5),
    }


def reference(x, c, ctx, c_ctx, w_mod, b_mod, ln_g, ln_b, ev_w_in, ev_w_out, ev_sink,
              ev_w1, ev_w3, ev_w2, od_w_in, ssm_a_re, ssm_a_im, ssm_log_dt, ssm_b_re, ssm_b_im,
              ssm_c_re, ssm_c_im, ssm_d, ssm_w_glu, ssm_b_glu, gmlp_w_s, gmlp_b_s, od_w_out,
              moe_w_router, moe_w1, moe_w3, moe_w2):
    n_lat = x.shape[1]
    n_rows = n_lat // GRID_W
    rows = jnp.broadcast_to(jnp.arange(n_rows)[:, None], (n_rows, GRID_W)).reshape(-1)
    cols = jnp.broadcast_to(jnp.arange(GRID_W)[None, :], (n_rows, GRID_W)).reshape(-1)
    silu_c = jax.nn.silu(c)
    silu_cc = jax.nn.silu(c_ctx)[None, :]
    x_lat, x_ctx = x, ctx
    for layer in range(DEPTH):
        need_ctx = layer < DEPTH - 1
        li = layer // 2
        mod_l = (silu_c @ w_mod[layer] + b_mod[layer])[:, None, :]
        mod_c = (silu_cc @ w_mod[layer] + b_mod[layer])[:, None, :]
        sh1, sc1, g1, sh2, sc2, g2 = jnp.split(mod_l, 6, axis=-1)
        csh1, csc1, cg1, csh2, csc2, cg2 = jnp.split(mod_c, 6, axis=-1)
        h_lat = x_lat * (1.0 + sc1) + sh1
        h_ctx = x_ctx * (1.0 + csc1) + csh1
        if layer % 2 == 0:
            y_lat, y_ctx = even_mixer(h_lat, h_ctx, ev_w_in[li], ev_w_out[li], ev_sink[li],
                                      rows, cols, need_ctx)
        else:
            y_lat, y_ctx = odd_mixer(h_lat, h_ctx, od_w_in[li], ssm_a_re[li], ssm_a_im[li],
                                     ssm_log_dt[li], ssm_b_re[li], ssm_b_im[li], ssm_c_re[li],
                                     ssm_c_im[li], ssm_d[li], ssm_w_glu[li], ssm_b_glu[li],
                                     gmlp_w_s[li], gmlp_b_s[li], od_w_out[li], need_ctx)
        x_lat = layer_norm(ALPHA * x_lat + g1 * y_lat, ln_g[layer, 0], ln_b[layer, 0])
        h2_lat = x_lat * (1.0 + sc2) + sh2
        if need_ctx:
            x_ctx = layer_norm(ALPHA * x_ctx + cg1 * y_ctx, ln_g[layer, 0], ln_b[layer, 0])
            h2 = jnp.concatenate([h2_lat, x_ctx * (1.0 + csc2) + csh2], axis=1)
        else:
            h2 = h2_lat
        if layer % 2 == 0:
            f2 = dense_swiglu(h2, ev_w1[li], ev_w3[li], ev_w2[li])
        else:
            f2 = moe_swiglu(h2, moe_w_router[li], moe_w1[li], moe_w3[li], moe_w2[li])
        x_lat = layer_norm(ALPHA * x_lat + g2 * f2[:, :n_lat], ln_g[layer, 1], ln_b[layer, 1])
        if need_ctx:
            x_ctx = layer_norm(ALPHA * x_ctx + cg2 * f2[:, n_lat:], ln_g[layer, 1], ln_b[layer, 1])
    return x_lat
```

```python
import functools
import math

import jax
import jax.numpy as jnp
from jax import lax
from jax.experimental import pallas as pl
from jax.experimental.pallas import tpu as pltpu

F32 = jnp.float32
BF16 = jnp.bfloat16
HIGHEST = lax.Precision.HIGHEST

D_MODEL = 1024
DEPTH = 4
GRID_W = 64
HEAD_DIM = 64
FOURIER_W = 512
FOURIER_GROUP_W = 128
ATT_HEADS = 8
ATT_KV_HEADS = 2
ATT_GROUP = 4
Q_W = 512
KV_W = 128
ATT_BLOCK = 128
ROPE_BASE = 10000.0
SSM_W = 512
SSM_GROUP = 16
SSM_GROUPS = 32
SSM_STATE = 64
GMLP_W = 512
GMLP_GROUPS = 4
CHUNK = 128
N_EXPERTS = 8
ALPHA = (2 * DEPTH) ** 0.25
LN_EPS = 1e-5
NEG_INF = -1e30

LANE = 128
MOD_ROWS = 24
VMEM_LIMIT = 56 * 1024 * 1024


def _params(sem):
    return pltpu.CompilerParams(dimension_semantics=sem, vmem_limit_bytes=VMEM_LIMIT)


def _silu(x):
    return x * jax.nn.sigmoid(x)


def _layer_norm(z, g, b):
    mu = jnp.mean(z, axis=-1, keepdims=True)
    d = z - mu
    var = jnp.mean(d * d, axis=-1, keepdims=True)
    return d * lax.rsqrt(var + LN_EPS) * g + b


def _mod_spec(part, row_fn):
    return pl.BlockSpec((None, None, 1, D_MODEL), lambda i, *_: (part, row_fn(i), 0, 0))


def _row_spec(tm, width):
    return pl.BlockSpec((tm, width), lambda i, *_: (i, 0))


def _full_spec(shape):
    nd = len(shape)
    return pl.BlockSpec(shape, lambda *_: (0,) * nd)


def _mod_kernel(c_ref, w_ref, b_ref, o_ref):
    s = _silu(c_ref[...])
    o_ref[...] = jnp.dot(s, w_ref[...], preferred_element_type=F32, precision=HIGHEST) + b_ref[...]


def modulation(cvec, w_mod, b_mod):
    depth = w_mod.shape[0]
    out = pl.pallas_call(
        _mod_kernel,
        grid=(depth, 6),
        in_specs=[
            pl.BlockSpec((MOD_ROWS, D_MODEL), lambda l, j: (0, 0)),
            pl.BlockSpec((None, D_MODEL, D_MODEL), lambda l, j: (l, 0, j)),
            pl.BlockSpec((None, None, 1, D_MODEL), lambda l, j: (l, j, 0, 0)),
        ],
        out_specs=pl.BlockSpec((None, None, MOD_ROWS, D_MODEL), lambda l, j: (l, j, 0, 0)),
        out_shape=jax.ShapeDtypeStruct((depth, 6, MOD_ROWS, D_MODEL), F32),
        compiler_params=_params(("arbitrary", "arbitrary")),
        name="modulation",
    )(cvec, w_mod, b_mod.reshape(depth, 6, 1, D_MODEL))
    return out.reshape(depth, 6, MOD_ROWS, 1, D_MODEL)


def _rope_slab(x, cos, sin_lo, sin_hi):
    return (x * cos + pltpu.roll(x, LANE - 16, 1) * sin_lo + pltpu.roll(x, 16, 1) * sin_hi)


def _even_in_kernel(rope, x_ref, sc_ref, sh_ref, w_ref, *refs):
    if rope:
        cos_ref, slo_ref, shi_ref, a_ref, q_ref, k_ref, v_ref = refs
    else:
        a_ref, q_ref, k_ref, v_ref = refs
    h = (x_ref[...] * (1.0 + sc_ref[...]) + sh_ref[...]).astype(BF16)
    p = jnp.dot(h, w_ref[...], preferred_element_type=F32)
    a_ref[...] = p[:, :FOURIER_W].astype(BF16)
    v_ref[...] = p[:, FOURIER_W + Q_W + KV_W:].astype(BF16)
    n_qk = (Q_W + KV_W) // LANE
    for s in range(n_qk):
        slab = p[:, FOURIER_W + s * LANE:FOURIER_W + (s + 1) * LANE]
        if rope:
            slab = _rope_slab(slab, cos_ref[...], slo_ref[...], shi_ref[...])
        slab = slab.astype(BF16)
        if s < Q_W // LANE:
            q_ref[:, s * LANE:(s + 1) * LANE] = slab
        else:
            k_ref[...] = slab


def even_in_proj(x, mods, w_in, rope_tabs, tm, tiles_per_batch, row_fn):
    t = x.shape[0]
    rope = rope_tabs is not None
    in_specs = [_row_spec(tm, D_MODEL), _mod_spec(1, row_fn), _mod_spec(0, row_fn),
                _full_spec(w_in.shape)]
    args = [x, mods, mods, w_in]
    if rope:
        tab_spec = pl.BlockSpec((tm, LANE), lambda i: (i % tiles_per_batch, 0))
        in_specs += [tab_spec] * 3
        args += list(rope_tabs)
    return pl.pallas_call(
        functools.partial(_even_in_kernel, rope),
        grid=(t // tm,),
        in_specs=in_specs,
        out_specs=[_row_spec(tm, FOURIER_W), _row_spec(tm, Q_W), _row_spec(tm, KV_W),
                   _row_spec(tm, KV_W)],
        out_shape=[jax.ShapeDtypeStruct((t, FOURIER_W), BF16), jax.ShapeDtypeStruct((t, Q_W), BF16),
                   jax.ShapeDtypeStruct((t, KV_W), BF16), jax.ShapeDtypeStruct((t, KV_W), BF16)],
        compiler_params=_params(("parallel",)),
        name="even_in_proj",
    )(*args)


def _fourier_kernel(n, tr, scale, a_ref, cs_ref, w_ref, o_ref, r_ref):
    j = pl.program_id(1)

    @pl.when(j == 0)
    def _():
        rows = min(n, 512)

        def body(c, carry):
            r0 = pl.multiple_of(c * rows, rows)
            blk = a_ref[pl.ds(r0, rows), :]
            for g in range(FOURIER_W // FOURIER_GROUP_W):
                acs = jnp.dot(blk[:, g * LANE:(g + 1) * LANE], cs_ref[...],
                              preferred_element_type=F32)
                r_ref[pl.ds(r0, rows), g * LANE:(g + 1) * LANE] = acs[:, :LANE].astype(BF16)
                r_ref[pl.ds(n + r0, rows), g * LANE:(g + 1) * LANE] = acs[:, LANE:].astype(BF16)
            return carry

        lax.fori_loop(0, n // rows, body, 0)

    y = jnp.dot(w_ref[...], r_ref[...], preferred_element_type=F32)
    o_ref[...] = (y * scale).astype(BF16)


def fourier_mix(a, n, cs, w, tr):
    bsz = a.shape[0] // n
    nt = n // tr
    scale = 1.0 / math.sqrt(n * FOURIER_GROUP_W)
    return pl.pallas_call(
        functools.partial(_fourier_kernel, n, tr, scale),
        grid=(bsz, nt),
        in_specs=[pl.BlockSpec((n, FOURIER_W), lambda b, j: (b, 0)),
                  pl.BlockSpec(cs.shape, lambda b, j: (0, 0)),
                  pl.BlockSpec((tr, 2 * n), lambda b, j: (j, 0))],
        out_specs=pl.BlockSpec((tr, FOURIER_W), lambda b, j: (b * nt + j, 0)),
        out_shape=jax.ShapeDtypeStruct(a.shape, BF16),
        scratch_shapes=[pltpu.VMEM((2 * n, FOURIER_W), BF16)],
        compiler_params=_params(("parallel", "arbitrary")),
        name="fourier_mix",
    )(a, cs, w)


def _softmax_sink_pv(s, sink, v):
    m = jnp.maximum(jnp.max(s, axis=-1, keepdims=True), sink)
    e = jnp.exp(s - m)
    den = jnp.sum(e, axis=-1, keepdims=True) + jnp.exp(sink - m)
    p = (e / den).astype(BF16)
    return jnp.dot(p, v, preferred_element_type=F32)


def _win_attn_kernel(n_blk, sink_ref, q_ref, kp_ref, kc_ref, kn_ref, vp_ref, vc_ref, vn_ref,
                     kx_ref, vx_ref, o_ref):
    blk = pl.program_id(1)
    n_ctx = kx_ref.shape[0]
    width = 3 * ATT_BLOCK + n_ctx
    qi = lax.broadcasted_iota(jnp.int32, (ATT_BLOCK, width), 0)
    kj = lax.broadcasted_iota(jnp.int32, (ATT_BLOCK, width), 1)
    k_min = jnp.where(blk == 0, ATT_BLOCK, 0)
    k_max = jnp.where(blk == n_blk - 1, 2 * ATT_BLOCK, 3 * ATT_BLOCK)
    slack = jnp.minimum(ATT_BLOCK - jnp.abs(kj - ATT_BLOCK - qi), jnp.minimum(kj - k_min, k_max - 1 - kj))
    valid = jnp.where(kj >= 3 * ATT_BLOCK, 0, slack) >= 0
    scale = HEAD_DIM ** -0.5
    for h in range(ATT_KV_HEADS):
        cols = slice(h * HEAD_DIM, (h + 1) * HEAD_DIM)
        kb = jnp.concatenate([kp_ref[:, cols], kc_ref[:, cols], kn_ref[:, cols], kx_ref[:, cols]], axis=0)
        vb = jnp.concatenate([vp_ref[:, cols], vc_ref[:, cols], vn_ref[:, cols], vx_ref[:, cols]], axis=0)
        for g in range(ATT_GROUP):
            head = h * ATT_GROUP + g
            qh = q_ref[:, head * HEAD_DIM:(head + 1) * HEAD_DIM]
            s = lax.dot_general(qh, kb, (((1,), (1,)), ((), ())), preferred_element_type=F32) * scale
            s = jnp.where(valid, s, NEG_INF)
            o = _softmax_sink_pv(s, sink_ref[head], vb)
            o_ref[:, head * HEAD_DIM:(head + 1) * HEAD_DIM] = o.astype(BF16)


def window_attention(q, k, v, k_ctx, v_ctx, sink, n, n_ctx):
    bsz = q.shape[0] // n
    n_blk = n // ATT_BLOCK

    def kv_spec(off):
        return pl.BlockSpec((ATT_BLOCK, KV_W),
                            lambda b, j: (b * n_blk + jnp.clip(j + off, 0, n_blk - 1), 0))

    ctx_spec = pl.BlockSpec((n_ctx, KV_W), lambda b, j: (b, 0))
    return pl.pallas_call(
        functools.partial(_win_attn_kernel, n_blk),
        grid=(bsz, n_blk),
        in_specs=[pl.BlockSpec(memory_space=pltpu.SMEM),
                  pl.BlockSpec((ATT_BLOCK, Q_W), lambda b, j: (b * n_blk + j, 0)),
                  kv_spec(-1), kv_spec(0), kv_spec(1), kv_spec(-1), kv_spec(0), kv_spec(1),
                  ctx_spec, ctx_spec],
        out_specs=pl.BlockSpec((ATT_BLOCK, Q_W), lambda b, j: (b * n_blk + j, 0)),
        out_shape=jax.ShapeDtypeStruct(q.shape, BF16),
        compiler_params=_params(("parallel", "arbitrary")),
        name="window_attention",
    )(sink, q, k, k, k, v, v, v, k_ctx, v_ctx)


def _ctx_attn_kernel(sink_ref, q_ref, k_ref, v_ref, o_ref):
    scale = HEAD_DIM ** -0.5
    for h in range(ATT_KV_HEADS):
        cols = slice(h * HEAD_DIM, (h + 1) * HEAD_DIM)
        kb = k_ref[:, cols]
        vb = v_ref[:, cols]
        for g in range(ATT_GROUP):
            head = h * ATT_GROUP + g
            qh = q_ref[:, head * HEAD_DIM:(head + 1) * HEAD_DIM]
            s = lax.dot_general(qh, kb, (((1,), (1,)), ((), ())), preferred_element_type=F32) * scale
            o = _softmax_sink_pv(s, sink_ref[head], vb)
            o_ref[:, head * HEAD_DIM:(head + 1) * HEAD_DIM] = o.astype(BF16)


def context_attention(q, k, v, sink, n_ctx):
    bsz = q.shape[0] // n_ctx
    return pl.pallas_call(
        _ctx_attn_kernel,
        grid=(bsz,),
        in_specs=[pl.BlockSpec(memory_space=pltpu.SMEM), _row_spec(n_ctx, Q_W),
                  _row_spec(n_ctx, KV_W), _row_spec(n_ctx, KV_W)],
        out_specs=_row_spec(n_ctx, Q_W),
        out_shape=jax.ShapeDtypeStruct(q.shape, BF16),
        compiler_params=_params(("parallel",)),
        name="context_attention",
    )(sink, q, k, v)


def _even_out_kernel(f_ref, a_ref, x_ref, w_ref, g1_ref, sc_ref, sh_ref, lg_ref, lb_ref,
                     xo_ref, h_ref):
    half = w_ref.shape[0] // 2
    y = (jnp.dot(f_ref[...], w_ref[:half, :], preferred_element_type=F32)
         + jnp.dot(a_ref[...], w_ref[half:, :], preferred_element_type=F32))
    xn = _layer_norm(ALPHA * x_ref[...] + g1_ref[...] * y, lg_ref[...], lb_ref[...])
    xo_ref[...] = xn
    h_ref[...] = (xn * (1.0 + sc_ref[...]) + sh_ref[...]).astype(BF16)


def even_out_proj(f, a, x, w_out, mods, ln_g, ln_b, tm, row_fn):
    t = x.shape[0]
    vec = _full_spec((1, D_MODEL))
    return pl.pallas_call(
        _even_out_kernel,
        grid=(t // tm,),
        in_specs=[_row_spec(tm, FOURIER_W), _row_spec(tm, Q_W), _row_spec(tm, D_MODEL),
                  _full_spec(w_out.shape), _mod_spec(2, row_fn), _mod_spec(4, row_fn),
                  _mod_spec(3, row_fn), vec, vec],
        out_specs=[_row_spec(tm, D_MODEL), _row_spec(tm, D_MODEL)],
        out_shape=[jax.ShapeDtypeStruct((t, D_MODEL), F32), jax.ShapeDtypeStruct((t, D_MODEL), BF16)],
        compiler_params=_params(("parallel",)),
        name="even_out_proj",
    )(f, a, x, w_out, mods, mods, mods, ln_g, ln_b)


def _dense_ffn_kernel(h_ref, x_ref, w1_ref, w3_ref, w2_ref, g2_ref, lg_ref, lb_ref, o_ref, acc_ref):
    j = pl.program_id(1)

    @pl.when(j == 0)
    def _():
        acc_ref[...] = jnp.zeros_like(acc_ref)

    h = h_ref[...]
    he = _silu(jnp.dot(h, w1_ref[...], preferred_element_type=F32)) * jnp.dot(
        h, w3_ref[...], preferred_element_type=F32)
    acc_ref[...] += jnp.dot(he.astype(BF16), w2_ref[...], preferred_element_type=F32)

    @pl.when(j == pl.num_programs(1) - 1)
    def _():
        o_ref[...] = _layer_norm(ALPHA * x_ref[...] + g2_ref[...] * acc_ref[...], lg_ref[...], lb_ref[...])


def dense_ffn(h, x, w1, w3, w2, mods, ln_g, ln_b, tm, th, row_fn):
    t = x.shape[0]
    hidden = w1.shape[1]
    vec = pl.BlockSpec((1, D_MODEL), lambda i, j: (0, 0))
    return pl.pallas_call(
        _dense_ffn_kernel,
        grid=(t // tm, hidden // th),
        in_specs=[pl.BlockSpec((tm, D_MODEL), lambda i, j: (i, 0)),
                  pl.BlockSpec((tm, D_MODEL), lambda i, j: (i, 0)),
                  pl.BlockSpec((D_MODEL, th), lambda i, j: (0, j)),
                  pl.BlockSpec((D_MODEL, th), lambda i, j: (0, j)),
                  pl.BlockSpec((th, D_MODEL), lambda i, j: (j, 0)),
                  pl.BlockSpec((None, None, 1, D_MODEL), lambda i, j: (5, row_fn(i), 0, 0)),
                  vec, vec],
        out_specs=pl.BlockSpec((tm, D_MODEL), lambda i, j: (i, 0)),
        out_shape=jax.ShapeDtypeStruct((t, D_MODEL), F32),
        scratch_shapes=[pltpu.VMEM((tm, D_MODEL), F32)],
        compiler_params=_params(("parallel", "arbitrary")),
        name="dense_ffn",
    )(h, x, w1, w3, w2, mods, ln_g, ln_b)


def _odd_in_kernel(x_ref, sc_ref, sh_ref, wt_ref, wg_ref, ut_ref, gm_ref):
    h = (x_ref[...] * (1.0 + sc_ref[...]) + sh_ref[...]).astype(BF16)
    gm_ref[...] = jnp.dot(h, wg_ref[...], preferred_element_type=F32)
    for cc in range(ut_ref.shape[0]):
        hc = h[cc * CHUNK:(cc + 1) * CHUNK, :]
        ut = lax.dot_general(wt_ref[...], hc, (((1,), (1,)), ((), ())), preferred_element_type=F32)
        ut_ref[cc] = ut.astype(BF16)


def odd_in_proj(x, mods, w_ssm_t, w_gmlp, bsz, tm, row_fn):
    t = x.shape[0]
    n = t // bsz
    tiles_per_batch = n // tm
    cpt = tm // CHUNK
    return pl.pallas_call(
        _odd_in_kernel,
        grid=(t // tm,),
        in_specs=[_row_spec(tm, D_MODEL), _mod_spec(1, row_fn), _mod_spec(0, row_fn),
                  _full_spec(w_ssm_t.shape), _full_spec(w_gmlp.shape)],
        out_specs=[pl.BlockSpec((cpt, None, SSM_W, CHUNK),
                                lambda i: (i % tiles_per_batch, i // tiles_per_batch, 0, 0)),
                   _row_spec(tm, 2 * GMLP_W)],
        out_shape=[jax.ShapeDtypeStruct((n // CHUNK, bsz, SSM_W, CHUNK), BF16),
                   jax.ShapeDtypeStruct((t, 2 * GMLP_W), F32)],
        compiler_params=_params(("parallel",)),
        name="odd_in_proj",
    )(x, mods, mods, w_ssm_t, w_gmlp)


def _s5_kernel(n_cc, n_lc, bsz, uc_ref, ul_ref, m_ref, g_ref, e_ref, a_ref, yc_ref, yl_ref):
    n_ch = n_cc + n_lc
    slabs = []
    for s in range(SSM_GROUP):
        sc = uc_ref[:, :, s, :].reshape(n_cc * bsz, CHUNK)
        sl = ul_ref[:, :, s, :].reshape(n_lc * bsz, CHUNK)
        slabs.append(jnp.concatenate([sc, sl], axis=0))
    lhs = jnp.concatenate(slabs, axis=1)
    y = jnp.dot(lhs, m_ref[...], preferred_element_type=F32)
    sm = jnp.dot(lhs, g_ref[...], preferred_element_type=F32)
    a = a_ref[...]
    n_st = 2 * SSM_STATE

    def step(state, a1, a2, inc):
        return a1 * state + a2 * pltpu.roll(state, SSM_STATE, 1) + inc

    state = jnp.zeros((bsz, n_st), F32)
    hf = []
    for k in range(n_ch):
        hf.append(state)
        state = step(state, a[0:1, :], a[1:2, :], sm[k * bsz:(k + 1) * bsz, :n_st])
    order = list(range(n_cc - 1, -1, -1)) + list(range(n_ch - 1, n_cc - 1, -1))
    state = jnp.zeros((bsz, n_st), F32)
    hr = [None] * n_ch
    for k in order:
        hr[k] = state
        state = step(state, a[2:3, :], a[3:4, :], sm[k * bsz:(k + 1) * bsz, n_st:])
    h_in = jnp.concatenate([jnp.concatenate(hf, axis=0), jnp.concatenate(hr, axis=0)], axis=1)
    y = y + jnp.dot(h_in.astype(BF16), e_ref[...], preferred_element_type=F32)
    rc = n_cc * bsz
    for s in range(SSM_GROUP):
        cols = slice(s * CHUNK, (s + 1) * CHUNK)
        yc_ref[:, :, s, :] = y[:rc, cols].reshape(n_cc, bsz, CHUNK)
        yl_ref[:, :, s, :] = y[rc:, cols].reshape(n_lc, bsz, CHUNK)


def s5_scan(ut_ctx, ut_lat, m, g, e, a):
    n_cc, bsz = ut_ctx.shape[:2]
    n_lc = ut_lat.shape[0]

    def grp(nc):
        return pl.BlockSpec((nc, bsz, SSM_GROUP, CHUNK), lambda i: (0, 0, i, 0))

    def par(arr):
        return pl.BlockSpec((None,) + arr.shape[1:], lambda i: (i, 0, 0))

    return pl.pallas_call(
        functools.partial(_s5_kernel, n_cc, n_lc, bsz),
        grid=(SSM_GROUPS,),
        in_specs=[grp(n_cc), grp(n_lc), par(m), par(g), par(e), par(a)],
        out_specs=[grp(n_cc), grp(n_lc)],
        out_shape=[jax.ShapeDtypeStruct(ut_ctx.shape, F32), jax.ShapeDtypeStruct(ut_lat.shape, F32)],
        compiler_params=_params(("parallel",)),
        name="s5_scan",
    )(ut_ctx, ut_lat, m, g, e, a)


def s5_matrices(a_re, a_im, log_dt, b_re, b_im, c_re, c_im):
    t = CHUNK
    k_idx = jnp.arange(t + 1, dtype=F32)

    def one_dir(d):
        lam_re, lam_im = a_re[d].astype(F32), a_im[d].astype(F32)
        dt = jnp.exp(log_dt[d].astype(F32))[:, None]
        mag = jnp.exp(k_idx[:, None, None] * (lam_re * dt)[None])
        ang = k_idx[:, None, None] * (lam_im * dt)[None]
        pw_re, pw_im = mag * jnp.cos(ang), mag * jnp.sin(ang)
        nr, ni = pw_re[1] - 1.0, pw_im[1]
        den = lam_re * lam_re + lam_im * lam_im
        fr, fi = (nr * lam_re + ni * lam_im) / den, (ni * lam_re - nr * lam_im) / den
        br, bi = b_re[d].astype(F32), b_im[d].astype(F32)
        bb_re = fr[..., None] * br - fi[..., None] * bi
        bb_im = fr[..., None] * bi + fi[..., None] * br
        cr, ci = c_re[d].astype(F32), c_im[d].astype(F32)
        return pw_re, pw_im, bb_re, bb_im, cr, ci

    def conv_kernel(pw_re, pw_im, bb_re, bb_im, cr, ci):
        w_re = pw_re[:t, :, :, None] * bb_re[None] - pw_im[:t, :, :, None] * bb_im[None]
        w_im = pw_re[:t, :, :, None] * bb_im[None] + pw_im[:t, :, :, None] * bb_re[None]
        return (jnp.einsum('gtp,kgps->gkts', cr, w_re, precision=HIGHEST)
                - jnp.einsum('gtp,kgps->gkts', ci, w_im, precision=HIGHEST))

    fw, rv = one_dir(0), one_dir(1)
    kf, kr = conv_kernel(*fw), conv_kernel(*rv)
    kd = jnp.concatenate([kr[:, :0:-1], kf[:, :1] + kr[:, :1], kf[:, 1:],
                          jnp.zeros_like(kf[:, :1])], axis=1).astype(BF16)
    jj = jnp.arange(t)[:, None]
    ii = jnp.arange(t)[None, :]
    onehot = (jnp.arange(2 * t)[:, None, None] == (ii - jj + t - 1)[None]).astype(BF16)
    m = jnp.einsum('dji,gdts->gsjti', onehot, kd, preferred_element_type=BF16)
    m = m.reshape(SSM_GROUPS, SSM_GROUP * t, SSM_GROUP * t)

    def summaries(pw_re, pw_im, bb_re, bb_im, cr, ci, reverse):
        pj_re = pw_re[:t] if reverse else pw_re[t - 1::-1][:t]
        pj_im = pw_im[:t] if reverse else pw_im[t - 1::-1][:t]
        g_re = jnp.einsum('jgp,gps->gsjp', pj_re, bb_re) - jnp.einsum('jgp,gps->gsjp', pj_im, bb_im)
        g_im = jnp.einsum('jgp,gps->gsjp', pj_re, bb_im) + jnp.einsum('jgp,gps->gsjp', pj_im, bb_re)
        pi_re = pw_re[t:0:-1] if reverse else pw_re[1:]
        pi_im = pw_im[t:0:-1] if reverse else pw_im[1:]
        e_re = jnp.einsum('gtp,igp->gpti', cr, pi_re) - jnp.einsum('gtp,igp->gpti', ci, pi_im)
        e_im = -(jnp.einsum('gtp,igp->gpti', cr, pi_im) + jnp.einsum('gtp,igp->gpti', ci, pi_re))
        a1 = jnp.concatenate([pw_re[t], pw_re[t]], axis=-1)
        a2 = jnp.concatenate([-pw_im[t], pw_im[t]], axis=-1)
        return g_re, g_im, e_re, e_im, a1, a2

    gf = summaries(*fw, False)
    gr = summaries(*rv, True)
    g = jnp.concatenate([gf[0], gf[1], gr[0], gr[1]], axis=-1)
    g = g.reshape(SSM_GROUPS, SSM_GROUP * t, 4 * SSM_STATE).astype(BF16)
    e = jnp.concatenate([gf[2], gf[3], gr[2], gr[3]], axis=1)
    e = e.reshape(SSM_GROUPS, 4 * SSM_STATE, SSM_GROUP * t).astype(BF16)
    a = jnp.stack([gf[4], gf[5], gr[4], gr[5]], axis=1)
    return m, g, e, a


def _odd_out_kernel(yt_ref, ut_ref, gm_ref, x_ref, dsk_ref, wgt_ref, bgl_ref, ws_ref, bs_ref,
                    w_ref, wr_ref, g1_ref, sc_ref, sh_ref, lg_ref, lb_ref,
                    xo_ref, h_ref, ri_ref, rw_ref, mix_ref):
    for cc in range(yt_ref.shape[0]):
        rows = slice(cc * CHUNK, (cc + 1) * CHUNK)
        t_t = dsk_ref[...] * ut_ref[cc].astype(F32) + yt_ref[cc]
        g_t = jax.nn.gelu(t_t)
        z_t = jnp.dot(wgt_ref[...], g_t.astype(BF16), preferred_element_type=F32) + bgl_ref[...]
        s_t = g_t * jax.nn.sigmoid(z_t)
        mix_ref[rows, :SSM_W] = s_t.T.astype(BF16)
        uv = jax.nn.gelu(gm_ref[rows, :])
        v = uv[:, GMLP_W:]
        mu = jnp.mean(v, axis=-1, keepdims=True)
        dv = v - mu
        var = jnp.mean(dv * dv, axis=-1, keepdims=True)
        v = (dv * lax.rsqrt(var + LN_EPS)).astype(BF16)
        for hh in range(GMLP_GROUPS):
            cols = slice(hh * LANE, (hh + 1) * LANE)
            vs = jnp.dot(ws_ref[hh], v[:, cols], preferred_element_type=F32) + bs_ref[hh]
            mix_ref[rows, SSM_W + hh * LANE:SSM_W + (hh + 1) * LANE] = (uv[:, cols] * vs).astype(BF16)
    y = jnp.dot(mix_ref[...], w_ref[...], preferred_element_type=F32)
    xn = _layer_norm(ALPHA * x_ref[...] + g1_ref[...] * y, lg_ref[...], lb_ref[...])
    xo_ref[...] = xn
    h2 = xn * (1.0 + sc_ref[...]) + sh_ref[...]
    h_ref[...] = h2
    logits = jnp.dot(h2, wr_ref[...], preferred_element_type=F32, precision=HIGHEST)
    lane = lax.broadcasted_iota(jnp.int32, logits.shape, 1)
    lg = jnp.where(lane < N_EXPERTS, logits, -jnp.inf)
    m1 = jnp.max(lg, axis=-1, keepdims=True)
    i1 = jnp.min(jnp.where(lg == m1, lane, LANE), axis=-1, keepdims=True)
    lg2 = jnp.where(lane == i1, -jnp.inf, lg)
    m2 = jnp.max(lg2, axis=-1, keepdims=True)
    i2 = jnp.min(jnp.where(lg2 == m2, lane, LANE), axis=-1, keepdims=True)
    e2 = jnp.exp(m2 - m1)
    w1 = 1.0 / (1.0 + e2)
    w2 = e2 / (1.0 + e2)
    ri_ref[...] = jnp.where(lane == 0, i1, jnp.where(lane == 1, i2, 0))
    rw_ref[...] = jnp.where(lane == 0, w1, jnp.where(lane == 1, w2, 0.0))


def odd_out_proj(yt, ut, gm, x, d_skip, w_glu_t, b_glu, w_s, b_s, w_out, w_router, mods,
                 ln_g, ln_b, bsz, tm, row_fn):
    t = x.shape[0]
    n = t // bsz
    tiles_per_batch = n // tm
    cpt = tm // CHUNK
    vec = _full_spec((1, D_MODEL))
    chunk_spec = pl.BlockSpec((cpt, None, SSM_W, CHUNK),
                              lambda i: (i % tiles_per_batch, i // tiles_per_batch, 0, 0))
    return pl.pallas_call(
        _odd_out_kernel,
        grid=(t // tm,),
        in_specs=[chunk_spec, chunk_spec, _row_spec(tm, 2 * GMLP_W), _row_spec(tm, D_MODEL),
                  _full_spec(d_skip.shape), _full_spec(w_glu_t.shape), _full_spec(b_glu.shape),
                  _full_spec(w_s.shape), _full_spec(b_s.shape), _full_spec(w_out.shape),
                  _full_spec(w_router.shape), _mod_spec(2, row_fn), _mod_spec(4, row_fn),
                  _mod_spec(3, row_fn), vec, vec],
        out_specs=[_row_spec(tm, D_MODEL), _row_spec(tm, D_MODEL), _row_spec(tm, LANE),
                   _row_spec(tm, LANE)],
        out_shape=[jax.ShapeDtypeStruct((t, D_MODEL), F32), jax.ShapeDtypeStruct((t, D_MODEL), F32),
                   jax.ShapeDtypeStruct((t, LANE), jnp.int32), jax.ShapeDtypeStruct((t, LANE), F32)],
        scratch_shapes=[pltpu.VMEM((tm, D_MODEL), BF16)],
        compiler_params=_params(("parallel",)),
        name="odd_out_proj",
    )(yt, ut, gm, x, d_skip, w_glu_t, b_glu, w_s, b_s, w_out, w_router, mods, mods, mods, ln_g, ln_b)


def _gather_rows_kernel(idx_ref, src_ref, o_ref, sem):
    tm = o_ref.shape[0]

    def issue(r, carry):
        pltpu.make_async_copy(src_ref.at[pl.ds(idx_ref[0, 0, r], 1), :],
                              o_ref.at[pl.ds(r, 1), :], sem).start()
        return carry

    lax.fori_loop(0, tm, issue, 0)

    def drain(r, carry):
        pltpu.make_async_copy(src_ref.at[pl.ds(0, 1), :], o_ref.at[pl.ds(r, 1), :], sem).wait()
        return carry

    lax.fori_loop(0, tm, drain, 0)


def gather_rows(src, idx, tm):
    r = idx.shape[0]
    width = src.shape[1]
    return pl.pallas_call(
        _gather_rows_kernel,
        grid=(r // tm,),
        in_specs=[pl.BlockSpec((1, 1, tm), lambda i: (i, 0, 0), memory_space=pltpu.SMEM),
                  pl.BlockSpec(memory_space=pl.ANY)],
        out_specs=_row_spec(tm, width),
        out_shape=jax.ShapeDtypeStruct((r, width), src.dtype),
        scratch_shapes=[pltpu.SemaphoreType.DMA],
        compiler_params=_params(("arbitrary",)),
        name="gather_rows",
    )(idx.reshape(r // tm, 1, tm), src)


def _expert_ffn_kernel(te_ref, tv_ref, x_ref, w1_ref, w3_ref, w2_ref, o_ref):
    i = pl.program_id(0)
    j = pl.program_id(1)

    @pl.when(j == 0)
    def _():
        o_ref[...] = jnp.zeros_like(o_ref)

    @pl.when(tv_ref[i] > 0)
    def _():
        h = x_ref[...].astype(BF16)
        he = _silu(jnp.dot(h, w1_ref[...], preferred_element_type=F32)) * jnp.dot(
            h, w3_ref[...], preferred_element_type=F32)
        o_ref[...] += jnp.dot(he.astype(BF16), w2_ref[...], preferred_element_type=F32)


def expert_ffn(xs, tile_expert, tile_valid, w1, w3, w2, tm, th):
    r = xs.shape[0]
    hidden = w1.shape[2]
    grid_spec = pltpu.PrefetchScalarGridSpec(
        num_scalar_prefetch=2,
        grid=(r // tm, hidden // th),
        in_specs=[pl.BlockSpec((tm, D_MODEL), lambda i, j, te, tv: (i, 0)),
                  pl.BlockSpec((None, D_MODEL, th), lambda i, j, te, tv: (te[i], 0, j)),
                  pl.BlockSpec((None, D_MODEL, th), lambda i, j, te, tv: (te[i], 0, j)),
                  pl.BlockSpec((None, th, D_MODEL), lambda i, j, te, tv: (te[i], j, 0))],
        out_specs=pl.BlockSpec((tm, D_MODEL), lambda i, j, te, tv: (i, 0)),
    )
    return pl.pallas_call(
        _expert_ffn_kernel,
        grid_spec=grid_spec,
        out_shape=jax.ShapeDtypeStruct((r, D_MODEL), F32),
        compiler_params=_params(("arbitrary", "arbitrary")),
        name="expert_ffn",
    )(tile_expert, tile_valid, xs, w1, w3, w2)


def _combine_kernel(idx_ref, ys_ref, rw_ref, x_ref, g2_ref, lg_ref, lb_ref, o_ref, buf_ref, sem):
    tm = o_ref.shape[0]

    def issue(r, carry):
        for k in range(2):
            pltpu.make_async_copy(ys_ref.at[pl.ds(idx_ref[0, 0, 2 * r + k], 1), :],
                                  buf_ref.at[k, pl.ds(r, 1), :], sem).start()
        return carry

    lax.fori_loop(0, tm, issue, 0)

    def drain(r, carry):
        for k in range(2):
            pltpu.make_async_copy(ys_ref.at[pl.ds(0, 1), :], buf_ref.at[k, pl.ds(r, 1), :], sem).wait()
        return carry

    lax.fori_loop(0, tm, drain, 0)
    rw = rw_ref[...]
    f2 = rw[:, 0:1] * buf_ref[0] + rw[:, 1:2] * buf_ref[1]
    o_ref[...] = _layer_norm(ALPHA * x_ref[...] + g2_ref[...] * f2, lg_ref[...], lb_ref[...])


def moe_combine(ys, slot_rows, route_w, x, mods, ln_g, ln_b, tm, row_fn):
    t = x.shape[0]
    vec = _full_spec((1, D_MODEL))
    return pl.pallas_call(
        _combine_kernel,
        grid=(t // tm,),
        in_specs=[pl.BlockSpec((1, 1, 2 * tm), lambda i: (i, 0, 0), memory_space=pltpu.SMEM),
                  pl.BlockSpec(memory_space=pl.ANY), _row_spec(tm, LANE), _row_spec(tm, D_MODEL),
                  _mod_spec(5, row_fn), vec, vec],
        out_specs=_row_spec(tm, D_MODEL),
        out_shape=jax.ShapeDtypeStruct((t, D_MODEL), F32),
        scratch_shapes=[pltpu.VMEM((2, tm, D_MODEL), F32), pltpu.SemaphoreType.DMA],
        compiler_params=_params(("arbitrary",)),
        name="moe_combine",
    )(slot_rows.reshape(t // tm, 1, 2 * tm), ys, route_w, x, mods, ln_g, ln_b)


def route_plan(route_i, tm):
    t = route_i.shape[0]
    e_flat = route_i[:, :2].reshape(-1)
    n_slots = 2 * t
    n_tiles = n_slots // tm + N_EXPERTS
    onehot = (e_flat[:, None] == jnp.arange(N_EXPERTS)[None, :]).astype(jnp.int32)
    csum = jnp.cumsum(onehot, axis=0)
    rank = jnp.sum((csum - onehot) * onehot, axis=1)
    counts = csum[-1]
    tiles_per = (counts + tm - 1) // tm
    tile_end = jnp.cumsum(tiles_per)
    row_start = (tile_end - tiles_per) * tm
    dest = row_start[e_flat] + rank
    src_token = jnp.zeros((n_tiles * tm,), jnp.int32).at[dest].set(jnp.arange(n_slots, dtype=jnp.int32) // 2)
    tile_ids = jnp.arange(n_tiles)
    tile_valid = (tile_ids < tile_end[-1]).astype(jnp.int32)
    last = jnp.maximum(tile_end[-1] - 1, 0)
    tile_expert = jnp.searchsorted(tile_end, jnp.minimum(tile_ids, last), side='right').astype(jnp.int32)
    tile_expert = jnp.minimum(tile_expert, N_EXPERTS - 1)
    return src_token, dest.astype(jnp.int32), tile_expert, tile_valid


def _rope_tables(n):
    pos = jnp.arange(n, dtype=jnp.int32)
    rows, cols = (pos // GRID_W).astype(F32), (pos % GRID_W).astype(F32)
    quarter = HEAD_DIM // 4
    d = jnp.arange(HEAD_DIM)
    inv = ROPE_BASE ** (-(d % quarter).astype(F32) / quarter)
    ang = jnp.where(d[None, :] < HEAD_DIM // 2, rows[:, None], cols[:, None]) * inv[None, :]
    cos, sin = jnp.cos(ang), jnp.sin(ang)
    low = (d % (2 * quarter) < quarter)[None, :]
    tabs = (cos, jnp.where(low, -sin, 0.0), jnp.where(low, 0.0, sin))
    return tuple(jnp.tile(t, (1, LANE // HEAD_DIM)) for t in tabs)


def _dft_tables(n):
    idx = jnp.arange(n, dtype=jnp.int32)
    ang = ((idx[:, None] * idx[None, :]) % n).astype(F32) * (2.0 * math.pi / n)
    return jnp.cos(ang), jnp.sin(ang)


def _tile(total, want):
    return want if total % want == 0 else total


def kernel(x, c, ctx, c_ctx, w_mod, b_mod, ln_g, ln_b, ev_w_in, ev_w_out, ev_sink, ev_w1, ev_w3, ev_w2,
           od_w_in, ssm_a_re, ssm_a_im, ssm_log_dt, ssm_b_re, ssm_b_im, ssm_c_re, ssm_c_im, ssm_d,
           ssm_w_glu, ssm_b_glu, gmlp_w_s, gmlp_b_s, od_w_out, moe_w_router, moe_w1, moe_w3, moe_w2):
    bsz, n, d = x.shape
    n_ctx = ctx.shape[1]
    depth = w_mod.shape[0]
    assert d == D_MODEL and bsz + 1 <= MOD_ROWS and n % CHUNK == 0 and n_ctx % CHUNK == 0
    xl = x.reshape(bsz * n, d)
    xc = ctx.reshape(bsz * n_ctx, d)
    cvec = jnp.zeros((MOD_ROWS, d), F32).at[:bsz].set(c).at[bsz].set(c_ctx)
    mods_all = modulation(cvec, w_mod, b_mod)

    tm_l = _tile(n, 1024)
    tm_f = _tile(n, 512)

    def lat_row(tm):
        return lambda i: i // (n // tm)

    def ctx_row(i):
        return bsz

    rope_tabs = _rope_tables(n)
    cc_c, sc_c = _dft_tables(FOURIER_GROUP_W)
    cs = jnp.concatenate([cc_c, sc_c], axis=1).astype(BF16)
    cn, sn = _dft_tables(n)
    w_n = jnp.concatenate([cn, -sn], axis=1).astype(BF16)
    cx, sx = _dft_tables(n_ctx)
    w_x = jnp.concatenate([cx, -sx], axis=1).astype(BF16)

    for layer in range(depth):
        need_ctx = layer < depth - 1
        li = layer // 2
        mods = mods_all[layer]
        lg1, lb1 = ln_g[layer, 0][None, :], ln_b[layer, 0][None, :]
        lg2, lb2 = ln_g[layer, 1][None, :], ln_b[layer, 1][None, :]
        if layer % 2 == 0:
            w_in = ev_w_in[li].astype(BF16)
            w_out = ev_w_out[li].astype(BF16)
            sink = ev_sink[li].astype(F32)
            a_l, q_l, k_l, v_l = even_in_proj(xl, mods, w_in, rope_tabs, tm_l, n // tm_l, lat_row(tm_l))
            a_c, q_c, k_c, v_c = even_in_proj(xc, mods, w_in, None, n_ctx, 1, ctx_row)
            f_l = fourier_mix(a_l, n, cs, w_n, tm_f)
            at_l = window_attention(q_l, k_l, v_l, k_c, v_c, sink, n, n_ctx)
            xl, h_l = even_out_proj(f_l, at_l, xl, w_out, mods, lg1, lb1, tm_l, lat_row(tm_l))
            w1, w3, w2 = ev_w1[li].astype(BF16), ev_w3[li].astype(BF16), ev_w2[li].astype(BF16)
            th = _tile(w1.shape[1], w1.shape[1] // 2)
            xl = dense_ffn(h_l, xl, w1, w3, w2, mods, lg2, lb2, tm_f, th, lat_row(tm_f))
            if need_ctx:
                f_c = fourier_mix(a_c, n_ctx, cs, w_x, n_ctx)
                at_c = context_attention(q_c, k_c, v_c, sink, n_ctx)
                xc, h_c = even_out_proj(f_c, at_c, xc, w_out, mods, lg1, lb1, n_ctx, ctx_row)
                xc = dense_ffn(h_c, xc, w1, w3, w2, mods, lg2, lb2, n_ctx, th, ctx_row)
        else:
            w_in = od_w_in[li]
            w_ssm_t = w_in[:, :SSM_W].T.astype(BF16)
            w_gmlp = w_in[:, SSM_W:].astype(BF16)
            ut_l, gm_l = odd_in_proj(xl, mods, w_ssm_t, w_gmlp, bsz, tm_l, lat_row(tm_l))
            ut_c, gm_c = odd_in_proj(xc, mods, w_ssm_t, w_gmlp, bsz, n_ctx, ctx_row)
            m, g, e, a = s5_matrices(ssm_a_re[li], ssm_a_im[li], ssm_log_dt[li], ssm_b_re[li],
                                     ssm_b_im[li], ssm_c_re[li], ssm_c_im[li])
            yt_c, yt_l = s5_scan(ut_c, ut_l, m, g, e, a)
            post = (ssm_d[li].astype(F32)[:, None], ssm_w_glu[li].T.astype(BF16),
                    ssm_b_glu[li].astype(F32)[:, None], gmlp_w_s[li].astype(BF16),
                    gmlp_b_s[li].astype(F32)[:, :, None], od_w_out[li].astype(BF16),
                    jnp.pad(moe_w_router[li].astype(F32), ((0, 0), (0, LANE - N_EXPERTS))))
            xl, h_l, ri_l, rw_l = odd_out_proj(yt_l, ut_l, gm_l, xl, *post, mods, lg1, lb1, bsz, tm_l,
                                               lat_row(tm_l))
            if need_ctx:
                xc, h_c, ri_c, rw_c = odd_out_proj(yt_c, ut_c, gm_c, xc, *post, mods, lg1, lb1, bsz,
                                                   n_ctx, ctx_row)
                h_all = jnp.concatenate([h_l, h_c], axis=0)
                ri_all = jnp.concatenate([ri_l, ri_c], axis=0)
            else:
                h_all, ri_all = h_l, ri_l
            tm_e = _tile(2 * h_all.shape[0], 1024)
            tm_g = _tile(tm_e, 512)
            src_token, slot_rows, tile_expert, tile_valid = route_plan(ri_all, tm_e)
            xs = gather_rows(h_all, src_token, tm_g)
            w1, w3, w2 = moe_w1[li].astype(BF16), moe_w3[li].astype(BF16), moe_w2[li].astype(BF16)
            th = _tile(w1.shape[2], w1.shape[2] // 4)
            ys = expert_ffn(xs, tile_expert, tile_valid, w1, w3, w2, tm_e, th)
            tm_c = _tile(n, 512)
            n_lat_slots = 2 * xl.shape[0]
            xl = moe_combine(ys, slot_rows[:n_lat_slots], rw_l, xl, mods, lg2, lb2, tm_c, lat_row(tm_c))
            if need_ctx:
                xc = moe_combine(ys, slot_rows[n_lat_slots:], rw_c, xc, mods, lg2, lb2,
                                 _tile(n_ctx, 512), ctx_row)
    return xl.reshape(bsz, n, d)
```

```python
import functools
import math

import jax
import jax.numpy as jnp
from jax import lax
from jax.experimental import pallas as pl
from jax.experimental.pallas import tpu as pltpu

F32 = jnp.float32
BF16 = jnp.bfloat16
HIGHEST = lax.Precision.HIGHEST

D_MODEL = 1024
DEPTH = 4
GRID_W = 64
HEAD_DIM = 64
FOURIER_W = 512
FOURIER_GROUP_W = 128
ATT_HEADS = 8
ATT_KV_HEADS = 2
ATT_GROUP = 4
Q_W = 512
KV_W = 128
ATT_BLOCK = 128
ROPE_BASE = 10000.0
SSM_W = 512
SSM_GROUP = 16
SSM_GROUPS = 32
SSM_STATE = 64
GMLP_W = 512
GMLP_GROUPS = 4
CHUNK = 128
N_EXPERTS = 8
ALPHA = (2 * DEPTH) ** 0.25
LN_EPS = 1e-5
NEG_INF = -1e30
LOG2E = math.log2(math.e)
QK_SCALE = LOG2E * HEAD_DIM ** -0.5

LANE = 128
MOD_ROWS = 24
VMEM_LIMIT = 56 * 1024 * 1024


def _params(sem, disable_bounds_checks=False):
    return pltpu.CompilerParams(dimension_semantics=sem, vmem_limit_bytes=VMEM_LIMIT,
                                disable_bounds_checks=disable_bounds_checks)


def _silu(x):
    return x * jax.nn.sigmoid(x)


def _layer_norm(z, g, b):
    mu = jnp.mean(z, axis=-1, keepdims=True)
    d = z - mu
    var = jnp.mean(d * d, axis=-1, keepdims=True)
    return d * lax.rsqrt(var + LN_EPS) * g + b


def _mod_spec(part, row_fn):
    return pl.BlockSpec((None, None, 1, D_MODEL), lambda i, *_: (part, row_fn(i), 0, 0))


def _row_spec(tm, width):
    return pl.BlockSpec((tm, width), lambda i, *_: (i, 0))


def _full_spec(shape):
    nd = len(shape)
    return pl.BlockSpec(shape, lambda *_: (0,) * nd)


def _mod_kernel(c_ref, w_ref, b_ref, o_ref):
    s = _silu(c_ref[...])
    o_ref[...] = jnp.dot(s, w_ref[...], preferred_element_type=F32, precision=HIGHEST) + b_ref[...]


def modulation(cvec, w_mod, b_mod):
    depth = w_mod.shape[0]
    out = pl.pallas_call(
        _mod_kernel,
        grid=(depth, 6),
        in_specs=[
            pl.BlockSpec((MOD_ROWS, D_MODEL), lambda l, j: (0, 0)),
            pl.BlockSpec((None, D_MODEL, D_MODEL), lambda l, j: (l, 0, j)),
            pl.BlockSpec((None, None, 1, D_MODEL), lambda l, j: (l, j, 0, 0)),
        ],
        out_specs=pl.BlockSpec((None, None, MOD_ROWS, D_MODEL), lambda l, j: (l, j, 0, 0)),
        out_shape=jax.ShapeDtypeStruct((depth, 6, MOD_ROWS, D_MODEL), F32),
        compiler_params=_params(("arbitrary", "arbitrary")),
        name="modulation",
    )(cvec, w_mod, b_mod.reshape(depth, 6, 1, D_MODEL))
    return out.reshape(depth, 6, MOD_ROWS, 1, D_MODEL)


def _rope_slab(x, cos, sin_lo, sin_hi):
    return (x * cos + pltpu.roll(x, LANE - 16, 1) * sin_lo + pltpu.roll(x, 16, 1) * sin_hi)


def _even_in_kernel(rope, x_ref, sc_ref, sh_ref, w_ref, *refs):
    if rope:
        cos_ref, slo_ref, shi_ref, a_ref, q_ref, k_ref, v_ref = refs
    else:
        a_ref, q_ref, k_ref, v_ref = refs
    h = (x_ref[...] * (1.0 + sc_ref[...]) + sh_ref[...]).astype(BF16)
    p = jnp.dot(h, w_ref[...], preferred_element_type=F32)
    a_ref[...] = p[:, :FOURIER_W].astype(BF16)
    v_ref[...] = p[:, FOURIER_W + Q_W + KV_W:].astype(BF16)
    n_qk = (Q_W + KV_W) // LANE
    for s in range(n_qk):
        slab = p[:, FOURIER_W + s * LANE:FOURIER_W + (s + 1) * LANE]
        if rope:
            slab = _rope_slab(slab, cos_ref[...], slo_ref[...], shi_ref[...])
        if s < Q_W // LANE:
            q_ref[:, s * LANE:(s + 1) * LANE] = (slab * QK_SCALE).astype(BF16)
        else:
            k_ref[...] = slab.astype(BF16)


def even_in_proj(x, mods, w_in, rope_tabs, tm, tiles_per_batch, row_fn):
    t = x.shape[0]
    rope = rope_tabs is not None
    in_specs = [_row_spec(tm, D_MODEL), _mod_spec(1, row_fn), _mod_spec(0, row_fn),
                _full_spec(w_in.shape)]
    args = [x, mods, mods, w_in]
    if rope:
        tab_spec = pl.BlockSpec((tm, LANE), lambda i: (i % tiles_per_batch, 0))
        in_specs += [tab_spec] * 3
        args += list(rope_tabs)
    return pl.pallas_call(
        functools.partial(_even_in_kernel, rope),
        grid=(t // tm,),
        in_specs=in_specs,
        out_specs=[_row_spec(tm, FOURIER_W), _row_spec(tm, Q_W), _row_spec(tm, KV_W),
                   _row_spec(tm, KV_W)],
        out_shape=[jax.ShapeDtypeStruct((t, FOURIER_W), BF16), jax.ShapeDtypeStruct((t, Q_W), BF16),
                   jax.ShapeDtypeStruct((t, KV_W), BF16), jax.ShapeDtypeStruct((t, KV_W), BF16)],
        compiler_params=_params(("parallel",)),
        name="even_in_proj",
    )(*args)


def _fourier_kernel(n, tr, scale, a_ref, cs_ref, w_ref, o_ref, r_ref):
    j = pl.program_id(1)

    @pl.when(j == 0)
    def _():
        rows = min(n, 512)

        def body(c, carry):
            r0 = pl.multiple_of(c * rows, rows)
            blk = a_ref[pl.ds(r0, rows), :]
            for g in range(FOURIER_W // FOURIER_GROUP_W):
                acs = jnp.dot(blk[:, g * LANE:(g + 1) * LANE], cs_ref[...],
                              preferred_element_type=F32)
                r_ref[pl.ds(r0, rows), g * LANE:(g + 1) * LANE] = acs[:, :LANE].astype(BF16)
                r_ref[pl.ds(n + r0, rows), g * LANE:(g + 1) * LANE] = acs[:, LANE:].astype(BF16)
            return carry

        lax.fori_loop(0, n // rows, body, 0)

    y = jnp.dot(w_ref[...], r_ref[...], preferred_element_type=F32)
    o_ref[...] = (y * scale).astype(BF16)


def fourier_mix(a, n, cs, w, tr):
    bsz = a.shape[0] // n
    nt = n // tr
    scale = 1.0 / math.sqrt(n * FOURIER_GROUP_W)
    return pl.pallas_call(
        functools.partial(_fourier_kernel, n, tr, scale),
        grid=(bsz, nt),
        in_specs=[pl.BlockSpec((n, FOURIER_W), lambda b, j: (b, 0)),
                  pl.BlockSpec(cs.shape, lambda b, j: (0, 0)),
                  pl.BlockSpec((tr, 2 * n), lambda b, j: (j, 0))],
        out_specs=pl.BlockSpec((tr, FOURIER_W), lambda b, j: (b * nt + j, 0)),
        out_shape=jax.ShapeDtypeStruct(a.shape, BF16),
        scratch_shapes=[pltpu.VMEM((2 * n, FOURIER_W), BF16)],
        compiler_params=_params(("parallel", "arbitrary")),
        name="fourier_mix",
    )(a, cs, w)


def _gqa_heads(q_ref, sink_ref, kv_of, valid, o_ref):
    nq = q_ref.shape[0]
    outs = []
    for h in range(ATT_KV_HEADS):
        kb, vb = kv_of(h)
        heads = [h * ATT_GROUP + g for g in range(ATT_GROUP)]
        qh = jnp.concatenate([q_ref[:, hd * HEAD_DIM:(hd + 1) * HEAD_DIM] for hd in heads], axis=0)
        sink = jnp.concatenate([jnp.full((nq, 1), sink_ref[hd], F32) for hd in heads], axis=0)
        s = lax.dot_general(qh, kb, (((1,), (1,)), ((), ())), preferred_element_type=F32)
        if valid is not None:
            s = jnp.where(valid, s, NEG_INF)
        m = jnp.maximum(jnp.max(s, axis=-1, keepdims=True), sink)
        e = jnp.exp2(s - m)
        den = jnp.sum(e, axis=-1, keepdims=True) + jnp.exp2(sink - m)
        o = jnp.dot(e.astype(BF16), vb, preferred_element_type=F32) * (1.0 / den)
        outs += [o[g * nq:(g + 1) * nq] for g in range(ATT_GROUP)]
    o_ref[...] = jnp.concatenate(outs, axis=1).astype(BF16)


def _win_attn_kernel(n_blk, sink_ref, q_ref, kp_ref, kc_ref, kn_ref, vp_ref, vc_ref, vn_ref,
                     kx_ref, vx_ref, o_ref):
    blk = pl.program_id(1)
    n_ctx = kx_ref.shape[0]
    shape = (ATT_GROUP * ATT_BLOCK, 3 * ATT_BLOCK + n_ctx)
    qi = lax.broadcasted_iota(jnp.int32, shape, 0) & (ATT_BLOCK - 1)
    kj = lax.broadcasted_iota(jnp.int32, shape, 1)
    k_min = jnp.where(blk == 0, ATT_BLOCK, 0)
    k_max = jnp.where(blk == n_blk - 1, 2 * ATT_BLOCK, 3 * ATT_BLOCK)
    slack = jnp.minimum(ATT_BLOCK - jnp.abs(kj - ATT_BLOCK - qi), jnp.minimum(kj - k_min, k_max - 1 - kj))
    valid = jnp.where(kj >= 3 * ATT_BLOCK, 0, slack) >= 0

    def kv_of(h):
        cols = slice(h * HEAD_DIM, (h + 1) * HEAD_DIM)
        kb = jnp.concatenate([kp_ref[:, cols], kc_ref[:, cols], kn_ref[:, cols], kx_ref[:, cols]], axis=0)
        vb = jnp.concatenate([vp_ref[:, cols], vc_ref[:, cols], vn_ref[:, cols], vx_ref[:, cols]], axis=0)
        return kb, vb

    _gqa_heads(q_ref, sink_ref, kv_of, valid, o_ref)


def window_attention(q, k, v, k_ctx, v_ctx, sink, n, n_ctx):
    bsz = q.shape[0] // n
    n_blk = n // ATT_BLOCK

    def kv_spec(off):
        return pl.BlockSpec((ATT_BLOCK, KV_W),
                            lambda b, j: (b * n_blk + jnp.clip(j + off, 0, n_blk - 1), 0))

    ctx_spec = pl.BlockSpec((n_ctx, KV_W), lambda b, j: (b, 0))
    return pl.pallas_call(
        functools.partial(_win_attn_kernel, n_blk),
        grid=(bsz, n_blk),
        in_specs=[pl.BlockSpec(memory_space=pltpu.SMEM),
                  pl.BlockSpec((ATT_BLOCK, Q_W), lambda b, j: (b * n_blk + j, 0)),
                  kv_spec(-1), kv_spec(0), kv_spec(1), kv_spec(-1), kv_spec(0), kv_spec(1),
                  ctx_spec, ctx_spec],
        out_specs=pl.BlockSpec((ATT_BLOCK, Q_W), lambda b, j: (b * n_blk + j, 0)),
        out_shape=jax.ShapeDtypeStruct(q.shape, BF16),
        compiler_params=_params(("parallel", "arbitrary")),
        name="window_attention",
    )(sink, q, k, k, k, v, v, v, k_ctx, v_ctx)


def _ctx_attn_kernel(sink_ref, q_ref, k_ref, v_ref, o_ref):
    def kv_of(h):
        cols = slice(h * HEAD_DIM, (h + 1) * HEAD_DIM)
        return k_ref[:, cols], v_ref[:, cols]

    _gqa_heads(q_ref, sink_ref, kv_of, None, o_ref)


def context_attention(q, k, v, sink, n_ctx):
    bsz = q.shape[0] // n_ctx
    return pl.pallas_call(
        _ctx_attn_kernel,
        grid=(bsz,),
        in_specs=[pl.BlockSpec(memory_space=pltpu.SMEM), _row_spec(n_ctx, Q_W),
                  _row_spec(n_ctx, KV_W), _row_spec(n_ctx, KV_W)],
        out_specs=_row_spec(n_ctx, Q_W),
        out_shape=jax.ShapeDtypeStruct(q.shape, BF16),
        compiler_params=_params(("parallel",)),
        name="context_attention",
    )(sink, q, k, v)


def _even_out_kernel(f_ref, a_ref, x_ref, w_ref, g1_ref, sc_ref, sh_ref, lg_ref, lb_ref,
                     xo_ref, h_ref):
    half = w_ref.shape[0] // 2
    y = (jnp.dot(f_ref[...], w_ref[:half, :], preferred_element_type=F32)
         + jnp.dot(a_ref[...], w_ref[half:, :], preferred_element_type=F32))
    xn = _layer_norm(ALPHA * x_ref[...] + g1_ref[...] * y, lg_ref[...], lb_ref[...])
    xo_ref[...] = xn
    h_ref[...] = (xn * (1.0 + sc_ref[...]) + sh_ref[...]).astype(BF16)


def even_out_proj(f, a, x, w_out, mods, ln_g, ln_b, tm, row_fn):
    t = x.shape[0]
    vec = _full_spec((1, D_MODEL))
    return pl.pallas_call(
        _even_out_kernel,
        grid=(t // tm,),
        in_specs=[_row_spec(tm, FOURIER_W), _row_spec(tm, Q_W), _row_spec(tm, D_MODEL),
                  _full_spec(w_out.shape), _mod_spec(2, row_fn), _mod_spec(4, row_fn),
                  _mod_spec(3, row_fn), vec, vec],
        out_specs=[_row_spec(tm, D_MODEL), _row_spec(tm, D_MODEL)],
        out_shape=[jax.ShapeDtypeStruct((t, D_MODEL), F32), jax.ShapeDtypeStruct((t, D_MODEL), BF16)],
        compiler_params=_params(("parallel",)),
        name="even_out_proj",
    )(f, a, x, w_out, mods, mods, mods, ln_g, ln_b)


def _dense_ffn_kernel(h_ref, x_ref, w1_ref, w3_ref, w2_ref, g2_ref, lg_ref, lb_ref, o_ref, acc_ref):
    j = pl.program_id(1)

    @pl.when(j == 0)
    def _():
        acc_ref[...] = jnp.zeros_like(acc_ref)

    h = h_ref[...]
    he = _silu(jnp.dot(h, w1_ref[...], preferred_element_type=F32)) * jnp.dot(
        h, w3_ref[...], preferred_element_type=F32)
    acc_ref[...] += jnp.dot(he.astype(BF16), w2_ref[...], preferred_element_type=F32)

    @pl.when(j == pl.num_programs(1) - 1)
    def _():
        o_ref[...] = _layer_norm(ALPHA * x_ref[...] + g2_ref[...] * acc_ref[...], lg_ref[...], lb_ref[...])


def dense_ffn(h, x, w1, w3, w2, mods, ln_g, ln_b, tm, th, row_fn):
    t = x.shape[0]
    hidden = w1.shape[1]
    vec = pl.BlockSpec((1, D_MODEL), lambda i, j: (0, 0))
    return pl.pallas_call(
        _dense_ffn_kernel,
        grid=(t // tm, hidden // th),
        in_specs=[pl.BlockSpec((tm, D_MODEL), lambda i, j: (i, 0)),
                  pl.BlockSpec((tm, D_MODEL), lambda i, j: (i, 0)),
                  pl.BlockSpec((D_MODEL, th), lambda i, j: (0, j)),
                  pl.BlockSpec((D_MODEL, th), lambda i, j: (0, j)),
                  pl.BlockSpec((th, D_MODEL), lambda i, j: (j, 0)),
                  pl.BlockSpec((None, None, 1, D_MODEL), lambda i, j: (5, row_fn(i), 0, 0)),
                  vec, vec],
        out_specs=pl.BlockSpec((tm, D_MODEL), lambda i, j: (i, 0)),
        out_shape=jax.ShapeDtypeStruct((t, D_MODEL), F32),
        scratch_shapes=[pltpu.VMEM((tm, D_MODEL), F32)],
        compiler_params=_params(("parallel", "arbitrary")),
        name="dense_ffn",
    )(h, x, w1, w3, w2, mods, ln_g, ln_b)


def _odd_in_kernel(x_ref, sc_ref, sh_ref, wt_ref, wg_ref, ut_ref, gm_ref):
    h = (x_ref[...] * (1.0 + sc_ref[...]) + sh_ref[...]).astype(BF16)
    gm_ref[...] = jnp.dot(h, wg_ref[...], preferred_element_type=F32)
    for cc in range(ut_ref.shape[0]):
        hc = h[cc * CHUNK:(cc + 1) * CHUNK, :]
        ut = lax.dot_general(wt_ref[...], hc, (((1,), (1,)), ((), ())), preferred_element_type=F32)
        ut_ref[cc] = ut.astype(BF16)


def odd_in_proj(x, mods, w_ssm_t, w_gmlp, bsz, tm, row_fn):
    t = x.shape[0]
    n = t // bsz
    tiles_per_batch = n // tm
    cpt = tm // CHUNK
    return pl.pallas_call(
        _odd_in_kernel,
        grid=(t // tm,),
        in_specs=[_row_spec(tm, D_MODEL), _mod_spec(1, row_fn), _mod_spec(0, row_fn),
                  _full_spec(w_ssm_t.shape), _full_spec(w_gmlp.shape)],
        out_specs=[pl.BlockSpec((cpt, None, SSM_W, CHUNK),
                                lambda i: (i % tiles_per_batch, i // tiles_per_batch, 0, 0)),
                   _row_spec(tm, 2 * GMLP_W)],
        out_shape=[jax.ShapeDtypeStruct((n // CHUNK, bsz, SSM_W, CHUNK), BF16),
                   jax.ShapeDtypeStruct((t, 2 * GMLP_W), F32)],
        compiler_params=_params(("parallel",)),
        name="odd_in_proj",
    )(x, mods, mods, w_ssm_t, w_gmlp)


def _s5_kernel(n_cc, n_lc, bsz, uc_ref, ul_ref, m_ref, g_ref, e_ref, a_ref, yc_ref, yl_ref):
    n_ch = n_cc + n_lc
    slabs = []
    for s in range(SSM_GROUP):
        sc = uc_ref[:, :, s, :].reshape(n_cc * bsz, CHUNK)
        sl = ul_ref[:, :, s, :].reshape(n_lc * bsz, CHUNK)
        slabs.append(jnp.concatenate([sc, sl], axis=0))
    lhs = jnp.concatenate(slabs, axis=1)
    y = jnp.dot(lhs, m_ref[...], preferred_element_type=F32)
    sm = jnp.dot(lhs, g_ref[...], preferred_element_type=F32)
    a = a_ref[...]
    n_st = 2 * SSM_STATE

    def step(state, a1, a2, inc):
        return a1 * state + a2 * pltpu.roll(state, SSM_STATE, 1) + inc

    state = jnp.zeros((bsz, n_st), F32)
    hf = []
    for k in range(n_ch):
        hf.append(state)
        state = step(state, a[0:1, :], a[1:2, :], sm[k * bsz:(k + 1) * bsz, :n_st])
    order = list(range(n_cc - 1, -1, -1)) + list(range(n_ch - 1, n_cc - 1, -1))
    state = jnp.zeros((bsz, n_st), F32)
    hr = [None] * n_ch
    for k in order:
        hr[k] = state
        state = step(state, a[2:3, :], a[3:4, :], sm[k * bsz:(k + 1) * bsz, n_st:])
    h_in = jnp.concatenate([jnp.concatenate(hf, axis=0), jnp.concatenate(hr, axis=0)], axis=1)
    y = y + jnp.dot(h_in.astype(BF16), e_ref[...], preferred_element_type=F32)
    rc = n_cc * bsz
    for s in range(SSM_GROUP):
        cols = slice(s * CHUNK, (s + 1) * CHUNK)
        yc_ref[:, :, s, :] = y[:rc, cols].reshape(n_cc, bsz, CHUNK)
        yl_ref[:, :, s, :] = y[rc:, cols].reshape(n_lc, bsz, CHUNK)


def s5_scan(ut_ctx, ut_lat, m, g, e, a):
    n_cc, bsz = ut_ctx.shape[:2]
    n_lc = ut_lat.shape[0]

    def grp(nc):
        return pl.BlockSpec((nc, bsz, SSM_GROUP, CHUNK), lambda i: (0, 0, i, 0))

    def par(arr):
        return pl.BlockSpec((None,) + arr.shape[1:], lambda i: (i, 0, 0))

    return pl.pallas_call(
        functools.partial(_s5_kernel, n_cc, n_lc, bsz),
        grid=(SSM_GROUPS,),
        in_specs=[grp(n_cc), grp(n_lc), par(m), par(g), par(e), par(a)],
        out_specs=[grp(n_cc), grp(n_lc)],
        out_shape=[jax.ShapeDtypeStruct(ut_ctx.shape, F32), jax.ShapeDtypeStruct(ut_lat.shape, F32)],
        compiler_params=_params(("parallel",)),
        name="s5_scan",
    )(ut_ctx, ut_lat, m, g, e, a)


def s5_matrices(a_re, a_im, log_dt, b_re, b_im, c_re, c_im):
    t = CHUNK
    k_idx = jnp.arange(t + 1, dtype=F32)

    def one_dir(d):
        lam_re, lam_im = a_re[d].astype(F32), a_im[d].astype(F32)
        dt = jnp.exp(log_dt[d].astype(F32))[:, None]
        mag = jnp.exp(k_idx[:, None, None] * (lam_re * dt)[None])
        ang = k_idx[:, None, None] * (lam_im * dt)[None]
        pw_re, pw_im = mag * jnp.cos(ang), mag * jnp.sin(ang)
        nr, ni = pw_re[1] - 1.0, pw_im[1]
        den = lam_re * lam_re + lam_im * lam_im
        fr, fi = (nr * lam_re + ni * lam_im) / den, (ni * lam_re - nr * lam_im) / den
        br, bi = b_re[d].astype(F32), b_im[d].astype(F32)
        bb_re = fr[..., None] * br - fi[..., None] * bi
        bb_im = fr[..., None] * bi + fi[..., None] * br
        cr, ci = c_re[d].astype(F32), c_im[d].astype(F32)
        return pw_re, pw_im, bb_re, bb_im, cr, ci

    def conv_kernel(pw_re, pw_im, bb_re, bb_im, cr, ci):
        w_re = pw_re[:t, :, :, None] * bb_re[None] - pw_im[:t, :, :, None] * bb_im[None]
        w_im = pw_re[:t, :, :, None] * bb_im[None] + pw_im[:t, :, :, None] * bb_re[None]
        return (jnp.einsum('gtp,kgps->gkts', cr, w_re, precision=HIGHEST)
                - jnp.einsum('gtp,kgps->gkts', ci, w_im, precision=HIGHEST))

    fw, rv = one_dir(0), one_dir(1)
    kf, kr = conv_kernel(*fw), conv_kernel(*rv)
    kd = jnp.concatenate([kr[:, :0:-1], kf[:, :1] + kr[:, :1], kf[:, 1:],
                          jnp.zeros_like(kf[:, :1])], axis=1).astype(BF16)
    jj = jnp.arange(t)[:, None]
    ii = jnp.arange(t)[None, :]
    onehot = (jnp.arange(2 * t)[:, None, None] == (ii - jj + t - 1)[None]).astype(BF16)
    m = jnp.einsum('dji,gdts->gsjti', onehot, kd, preferred_element_type=BF16)
    m = m.reshape(SSM_GROUPS, SSM_GROUP * t, SSM_GROUP * t)

    def summaries(pw_re, pw_im, bb_re, bb_im, cr, ci, reverse):
        pj_re = pw_re[:t] if reverse else pw_re[t - 1::-1][:t]
        pj_im = pw_im[:t] if reverse else pw_im[t - 1::-1][:t]
        g_re = jnp.einsum('jgp,gps->gsjp', pj_re, bb_re) - jnp.einsum('jgp,gps->gsjp', pj_im, bb_im)
        g_im = jnp.einsum('jgp,gps->gsjp', pj_re, bb_im) + jnp.einsum('jgp,gps->gsjp', pj_im, bb_re)
        pi_re = pw_re[t:0:-1] if reverse else pw_re[1:]
        pi_im = pw_im[t:0:-1] if reverse else pw_im[1:]
        e_re = jnp.einsum('gtp,igp->gpti', cr, pi_re) - jnp.einsum('gtp,igp->gpti', ci, pi_im)
        e_im = -(jnp.einsum('gtp,igp->gpti', cr, pi_im) + jnp.einsum('gtp,igp->gpti', ci, pi_re))
        a1 = jnp.concatenate([pw_re[t], pw_re[t]], axis=-1)
        a2 = jnp.concatenate([-pw_im[t], pw_im[t]], axis=-1)
        return g_re, g_im, e_re, e_im, a1, a2

    gf = summaries(*fw, False)
    gr = summaries(*rv, True)
    g = jnp.concatenate([gf[0], gf[1], gr[0], gr[1]], axis=-1)
    g = g.reshape(SSM_GROUPS, SSM_GROUP * t, 4 * SSM_STATE).astype(BF16)
    e = jnp.concatenate([gf[2], gf[3], gr[2], gr[3]], axis=1)
    e = e.reshape(SSM_GROUPS, 4 * SSM_STATE, SSM_GROUP * t).astype(BF16)
    a = jnp.stack([gf[4], gf[5], gr[4], gr[5]], axis=1)
    return m, g, e, a


def _odd_out_kernel(yt_ref, ut_ref, gm_ref, x_ref, dsk_ref, wgt_ref, bgl_ref, ws_ref, bs_ref,
                    w_ref, wr_ref, g1_ref, sc_ref, sh_ref, lg_ref, lb_ref,
                    xo_ref, h_ref, ri_ref, rw_ref, mix_ref):
    for cc in range(yt_ref.shape[0]):
        rows = slice(cc * CHUNK, (cc + 1) * CHUNK)
        t_t = dsk_ref[...] * ut_ref[cc].astype(F32) + yt_ref[cc]
        g_t = jax.nn.gelu(t_t)
        z_t = jnp.dot(wgt_ref[...], g_t.astype(BF16), preferred_element_type=F32) + bgl_ref[...]
        s_t = g_t * jax.nn.sigmoid(z_t)
        mix_ref[rows, :SSM_W] = s_t.T.astype(BF16)
        uv = jax.nn.gelu(gm_ref[rows, :])
        v = uv[:, GMLP_W:]
        mu = jnp.mean(v, axis=-1, keepdims=True)
        dv = v - mu
        var = jnp.mean(dv * dv, axis=-1, keepdims=True)
        v = (dv * lax.rsqrt(var + LN_EPS)).astype(BF16)
        for hh in range(GMLP_GROUPS):
            cols = slice(hh * LANE, (hh + 1) * LANE)
            vs = jnp.dot(ws_ref[hh], v[:, cols], preferred_element_type=F32) + bs_ref[hh]
            mix_ref[rows, SSM_W + hh * LANE:SSM_W + (hh + 1) * LANE] = (uv[:, cols] * vs).astype(BF16)
    y = jnp.dot(mix_ref[...], w_ref[...], preferred_element_type=F32)
    xn = _layer_norm(ALPHA * x_ref[...] + g1_ref[...] * y, lg_ref[...], lb_ref[...])
    xo_ref[...] = xn
    h2 = xn * (1.0 + sc_ref[...]) + sh_ref[...]
    h_ref[...] = h2
    h_hi = h2.astype(BF16)
    h_lo = (h2 - h_hi.astype(F32)).astype(BF16)
    logits = (jnp.dot(h_hi, wr_ref[0], preferred_element_type=F32)
              + jnp.dot(h_lo, wr_ref[0], preferred_element_type=F32)
              + jnp.dot(h_hi, wr_ref[1], preferred_element_type=F32))
    lane =lax.broadcasted_iota(jnp.int32, logits.shape, 1)
    lg = jnp.where(lane < N_EXPERTS, logits, -jnp.inf)
    m1 = jnp.max(lg, axis=-1, keepdims=True)
    i1 = jnp.min(jnp.where(lg == m1, lane, LANE), axis=-1, keepdims=True)
    lg2 = jnp.where(lane == i1, -jnp.inf, lg)
    m2 = jnp.max(lg2, axis=-1, keepdims=True)
    i2 = jnp.min(jnp.where(lg2 == m2, lane, LANE), axis=-1, keepdims=True)
    e2 = jnp.exp(m2 - m1)
    w1 = 1.0 / (1.0 + e2)
    w2 = e2 / (1.0 + e2)
    ri_ref[...] = jnp.where(lane == 0, i1, jnp.where(lane == 1, i2, 0))
    rw_ref[...] = jnp.where(lane == 0, w1, jnp.where(lane == 1, w2, 0.0))


def odd_out_proj(yt, ut, gm, x, d_skip, w_glu_t, b_glu, w_s, b_s, w_out, w_router, mods,
                 ln_g, ln_b, bsz, tm, row_fn):
    t = x.shape[0]
    n = t // bsz
    tiles_per_batch = n // tm
    cpt = tm // CHUNK
    vec = _full_spec((1, D_MODEL))
    chunk_spec = pl.BlockSpec((cpt, None, SSM_W, CHUNK),
                              lambda i: (i % tiles_per_batch, i // tiles_per_batch, 0, 0))
    return pl.pallas_call(
        _odd_out_kernel,
        grid=(t // tm,),
        in_specs=[chunk_spec, chunk_spec, _row_spec(tm, 2 * GMLP_W), _row_spec(tm, D_MODEL),
                  _full_spec(d_skip.shape), _full_spec(w_glu_t.shape), _full_spec(b_glu.shape),
                  _full_spec(w_s.shape), _full_spec(b_s.shape), _full_spec(w_out.shape),
                  _full_spec(w_router.shape), _mod_spec(2, row_fn), _mod_spec(4, row_fn),
                  _mod_spec(3, row_fn), vec, vec],
        out_specs=[_row_spec(tm, D_MODEL), _row_spec(tm, D_MODEL), _row_spec(tm, LANE),
                   _row_spec(tm, LANE)],
        out_shape=[jax.ShapeDtypeStruct((t, D_MODEL), F32), jax.ShapeDtypeStruct((t, D_MODEL), F32),
                   jax.ShapeDtypeStruct((t, LANE), jnp.int32), jax.ShapeDtypeStruct((t, LANE), F32)],
        scratch_shapes=[pltpu.VMEM((tm, D_MODEL), BF16)],
        compiler_params=_params(("parallel",)),
        name="odd_out_proj",
    )(yt, ut, gm, x, d_skip, w_glu_t, b_glu, w_s, b_s, w_out, w_router, mods, mods, mods, ln_g, ln_b)


ROW_DMA_UNROLL = 8
ROW_WAIT_GROUP = 64


def _drain_rows(n_rows, src_row, dst_row, sem):
    def body(_, carry):
        for _ in range(ROW_WAIT_GROUP):
            pltpu.make_async_copy(src_row, dst_row, sem).wait()
        return carry

    lax.fori_loop(0, n_rows // ROW_WAIT_GROUP, body, 0)


def _dispatch_kernel(idx_ref, h_ref, xs_in_ref, xs_ref, sem):
    del xs_in_ref
    tm = h_ref.shape[0]

    def issue(r, carry):
        for k in range(2):
            pltpu.make_async_copy(h_ref.at[pl.ds(r, 1), :],
                                  xs_ref.at[pl.ds(idx_ref[0, 0, 2 * r + k], 1), :], sem).start()
        return carry

    lax.fori_loop(0, tm, issue, 0, unroll=ROW_DMA_UNROLL)
    _drain_rows(2 * tm, h_ref.at[pl.ds(0, 1), :], xs_ref.at[pl.ds(0, 1), :], sem)


def moe_dispatch(h, slot_rows, n_rows, tm):
    t, width = h.shape
    return pl.pallas_call(
        _dispatch_kernel,
        grid=(t // tm,),
        in_specs=[pl.BlockSpec((1, 1, 2 * tm), lambda i: (i, 0, 0), memory_space=pltpu.SMEM),
                  _row_spec(tm, width), pl.BlockSpec(memory_space=pl.ANY)],
        out_specs=pl.BlockSpec(memory_space=pl.ANY),
        out_shape=jax.ShapeDtypeStruct((n_rows, width), h.dtype),
        input_output_aliases={2: 0},
        scratch_shapes=[pltpu.SemaphoreType.DMA],
        compiler_params=_params(("arbitrary",), disable_bounds_checks=True),
        name="moe_dispatch",
    )(slot_rows.reshape(t // tm, 1, 2 * tm), h, jnp.zeros((n_rows, width), h.dtype))


def _expert_ffn_kernel(te_ref, tv_ref, x_ref, w1_ref, w3_ref, w2_ref, o_ref):
    i = pl.program_id(0)
    j = pl.program_id(1)

    @pl.when(j == 0)
    def _():
        o_ref[...] = jnp.zeros_like(o_ref)

    @pl.when(tv_ref[i] > 0)
    def _():
        h = x_ref[...].astype(BF16)
        he = _silu(jnp.dot(h, w1_ref[...], preferred_element_type=F32)) * jnp.dot(
            h, w3_ref[...], preferred_element_type=F32)
        o_ref[...] += jnp.dot(he.astype(BF16), w2_ref[...], preferred_element_type=F32)


def expert_ffn(xs, tile_expert, tile_valid, w1, w3, w2, tm, th):
    r = xs.shape[0]
    hidden = w1.shape[2]
    grid_spec = pltpu.PrefetchScalarGridSpec(
        num_scalar_prefetch=2,
        grid=(r // tm, hidden // th),
        in_specs=[pl.BlockSpec((tm, D_MODEL), lambda i, j, te, tv: (i, 0)),
                  pl.BlockSpec((None, D_MODEL, th), lambda i, j, te, tv: (te[i], 0, j)),
                  pl.BlockSpec((None, D_MODEL, th), lambda i, j, te, tv: (te[i], 0, j)),
                  pl.BlockSpec((None, th, D_MODEL), lambda i, j, te, tv: (te[i], j, 0))],
        out_specs=pl.BlockSpec((tm, D_MODEL), lambda i, j, te, tv: (i, 0)),
    )
    return pl.pallas_call(
        _expert_ffn_kernel,
        grid_spec=grid_spec,
        out_shape=jax.ShapeDtypeStruct((r, D_MODEL), F32),
        compiler_params=_params(("arbitrary", "arbitrary")),
        name="expert_ffn",
    )(tile_expert, tile_valid, xs, w1, w3, w2)


def _combine_kernel(idx_ref, ys_ref, rw_ref, x_ref, g2_ref, lg_ref, lb_ref, o_ref, buf_ref, sem):
    tm = o_ref.shape[0]

    def issue(r, carry):
        for k in range(2):
            pltpu.make_async_copy(ys_ref.at[pl.ds(idx_ref[0, 0, 2 * r + k], 1), :],
                                  buf_ref.at[k, pl.ds(r, 1), :], sem).start()
        return carry

    lax.fori_loop(0, tm, issue, 0, unroll=ROW_DMA_UNROLL)
    _drain_rows(2 * tm, ys_ref.at[pl.ds(0, 1), :], buf_ref.at[0, pl.ds(0, 1), :], sem)
    rw = rw_ref[...]
    f2 = rw[:, 0:1] * buf_ref[0] + rw[:, 1:2] * buf_ref[1]
    o_ref[...] = _layer_norm(ALPHA * x_ref[...] + g2_ref[...] * f2, lg_ref[...], lb_ref[...])


def moe_combine(ys, slot_rows, route_w, x, mods, ln_g, ln_b, tm, row_fn):
    t = x.shape[0]
    vec = _full_spec((1, D_MODEL))
    return pl.pallas_call(
        _combine_kernel,
        grid=(t // tm,),
        in_specs=[pl.BlockSpec((1, 1, 2 * tm), lambda i: (i, 0, 0), memory_space=pltpu.SMEM),
                  pl.BlockSpec(memory_space=pl.ANY), _row_spec(tm, LANE), _row_spec(tm, D_MODEL),
                  _mod_spec(5, row_fn), vec, vec],
        out_specs=_row_spec(tm, D_MODEL),
        out_shape=jax.ShapeDtypeStruct((t, D_MODEL), F32),
        scratch_shapes=[pltpu.VMEM((2, tm, D_MODEL), F32), pltpu.SemaphoreType.DMA],
        compiler_params=_params(("arbitrary",), disable_bounds_checks=True),
        name="moe_combine",
    )(slot_rows.reshape(t // tm, 1, 2 * tm), ys, route_w, x, mods, ln_g, ln_b)


def route_plan(route_i, tm):
    t = route_i.shape[0]
    e_flat = route_i[:, :2].reshape(-1)
    n_slots = 2 * t
    n_tiles = n_slots // tm + N_EXPERTS
    onehot = (e_flat[:, None] == jnp.arange(N_EXPERTS)[None, :]).astype(jnp.int32)
    csum = jnp.cumsum(onehot, axis=0)
    rank = jnp.sum((csum - onehot) * onehot, axis=1)
    counts = csum[-1]
    tiles_per = (counts + tm - 1) // tm
    tile_end = jnp.cumsum(tiles_per)
    row_start = (tile_end - tiles_per) * tm
    dest = row_start[e_flat] + rank
    tile_ids = jnp.arange(n_tiles)
    tile_valid = (tile_ids < tile_end[-1]).astype(jnp.int32)
    last = jnp.maximum(tile_end[-1] - 1, 0)
    tile_expert = jnp.searchsorted(tile_end, jnp.minimum(tile_ids, last), side='right').astype(jnp.int32)
    tile_expert = jnp.minimum(tile_expert, N_EXPERTS - 1)
    return dest.astype(jnp.int32), tile_expert, tile_valid


def _rope_tables(n):
    pos = jnp.arange(n, dtype=jnp.int32)
    rows, cols = (pos // GRID_W).astype(F32), (pos % GRID_W).astype(F32)
    quarter = HEAD_DIM // 4
    d = jnp.arange(HEAD_DIM)
    inv = ROPE_BASE ** (-(d % quarter).astype(F32) / quarter)
    ang = jnp.where(d[None, :] < HEAD_DIM // 2, rows[:, None], cols[:, None]) * inv[None, :]
    cos, sin = jnp.cos(ang), jnp.sin(ang)
    low = (d % (2 * quarter) < quarter)[None, :]
    tabs = (cos, jnp.where(low, -sin, 0.0), jnp.where(low, 0.0, sin))
    return tuple(jnp.tile(t, (1, LANE // HEAD_DIM)) for t in tabs)


def _dft_tables(n):
    idx = jnp.arange(n, dtype=jnp.int32)
    ang = ((idx[:, None] * idx[None, :]) % n).astype(F32) * (2.0 * math.pi / n)
    return jnp.cos(ang), jnp.sin(ang)


def _tile(total, want):
    return want if total % want == 0 else total


def kernel(x, c, ctx, c_ctx, w_mod, b_mod, ln_g, ln_b, ev_w_in, ev_w_out, ev_sink, ev_w1, ev_w3, ev_w2,
           od_w_in, ssm_a_re, ssm_a_im, ssm_log_dt, ssm_b_re, ssm_b_im, ssm_c_re, ssm_c_im, ssm_d,
           ssm_w_glu, ssm_b_glu, gmlp_w_s, gmlp_b_s, od_w_out, moe_w_router, moe_w1, moe_w3, moe_w2):
    bsz, n, d = x.shape
    n_ctx = ctx.shape[1]
    depth = w_mod.shape[0]
    assert d == D_MODEL and bsz + 1 <= MOD_ROWS and n % CHUNK == 0 and n_ctx % CHUNK == 0
    xl = x.reshape(bsz * n, d)
    xc = ctx.reshape(bsz * n_ctx, d)
    cvec = jnp.zeros((MOD_ROWS, d), F32).at[:bsz].set(c).at[bsz].set(c_ctx)
    mods_all = modulation(cvec, w_mod, b_mod)

    tm_l = _tile(n, 1024)
    tm_f = _tile(n, 512)

    def lat_row(tm):
        return lambda i: i // (n // tm)

    def ctx_row(i):
        return bsz

    rope_tabs = _rope_tables(n)
    cc_c, sc_c = _dft_tables(FOURIER_GROUP_W)
    cs = jnp.concatenate([cc_c, sc_c], axis=1).astype(BF16)
    cn, sn = _dft_tables(n)
    w_n = jnp.concatenate([cn, -sn], axis=1).astype(BF16)
    cx, sx = _dft_tables(n_ctx)
    w_x = jnp.concatenate([cx, -sx], axis=1).astype(BF16)

    for layer in range(depth):
        need_ctx = layer < depth - 1
        li = layer // 2
        mods = mods_all[layer]
        lg1, lb1 = ln_g[layer, 0][None, :], ln_b[layer, 0][None, :]
        lg2, lb2 = ln_g[layer, 1][None, :], ln_b[layer, 1][None, :]
        if layer % 2 == 0:
            w_in = ev_w_in[li].astype(BF16)
            w_out = ev_w_out[li].astype(BF16)
            sink = ev_sink[li].astype(F32) * LOG2E
            a_l, q_l, k_l, v_l = even_in_proj(xl, mods, w_in, rope_tabs, tm_l, n // tm_l, lat_row(tm_l))
            a_c, q_c, k_c, v_c = even_in_proj(xc, mods, w_in, None, n_ctx, 1, ctx_row)
            f_l = fourier_mix(a_l, n, cs, w_n, tm_f)
            at_l = window_attention(q_l, k_l, v_l, k_c, v_c, sink, n, n_ctx)
            xl, h_l = even_out_proj(f_l, at_l, xl, w_out, mods, lg1, lb1, tm_l, lat_row(tm_l))
            w1, w3, w2 = ev_w1[li].astype(BF16), ev_w3[li].astype(BF16), ev_w2[li].astype(BF16)
            th = _tile(w1.shape[1], w1.shape[1] // 2)
            xl = dense_ffn(h_l, xl, w1, w3, w2, mods, lg2, lb2, tm_f, th, lat_row(tm_f))
            if need_ctx:
                f_c = fourier_mix(a_c, n_ctx, cs, w_x, n_ctx)
                at_c = context_attention(q_c, k_c, v_c, sink, n_ctx)
                xc, h_c = even_out_proj(f_c, at_c, xc, w_out, mods, lg1, lb1, n_ctx, ctx_row)
                xc = dense_ffn(h_c, xc, w1, w3, w2, mods, lg2, lb2, n_ctx, th, ctx_row)
        else:
            w_in = od_w_in[li]
            w_ssm_t = w_in[:, :SSM_W].T.astype(BF16)
            w_gmlp = w_in[:, SSM_W:].astype(BF16)
            ut_l, gm_l = odd_in_proj(xl, mods, w_ssm_t, w_gmlp, bsz, tm_l, lat_row(tm_l))
            ut_c, gm_c = odd_in_proj(xc, mods, w_ssm_t, w_gmlp, bsz, n_ctx, ctx_row)
            m, g, e, a = s5_matrices(ssm_a_re[li], ssm_a_im[li], ssm_log_dt[li], ssm_b_re[li],
                                     ssm_b_im[li], ssm_c_re[li], ssm_c_im[li])
            yt_c, yt_l = s5_scan(ut_c, ut_l, m, g, e, a)
            w_r = jnp.pad(moe_w_router[li].astype(F32), ((0, 0), (0, LANE - N_EXPERTS)))
            w_r_hi = w_r.astype(BF16)
            w_r_split = jnp.stack([w_r_hi, (w_r - w_r_hi.astype(F32)).astype(BF16)])
            post = (ssm_d[li].astype(F32)[:, None], ssm_w_glu[li].T.astype(BF16),
                    ssm_b_glu[li].astype(F32)[:, None], gmlp_w_s[li].astype(BF16),
                    gmlp_b_s[li].astype(F32)[:, :, None], od_w_out[li].astype(BF16), w_r_split)
            xl, h_l, ri_l, rw_l = odd_out_proj(yt_l, ut_l, gm_l, xl, *post, mods, lg1, lb1, bsz, tm_l,
                                               lat_row(tm_l))
            if need_ctx:
                xc, h_c, ri_c, rw_c = odd_out_proj(yt_c, ut_c, gm_c, xc, *post, mods, lg1, lb1, bsz,
                                                   n_ctx, ctx_row)
                h_all = jnp.concatenate([h_l, h_c], axis=0)
                ri_all = jnp.concatenate([ri_l, ri_c], axis=0)
            else:
                h_all, ri_all = h_l, ri_l
            tm_e = _tile(2 * h_all.shape[0], 1024)
            slot_rows, tile_expert, tile_valid = route_plan(ri_all, tm_e)
            n_rows = (2 * h_all.shape[0] // tm_e + N_EXPERTS) * tm_e
            xs = moe_dispatch(h_all, slot_rows, n_rows, _tile(h_all.shape[0], 512))
            w1, w3, w2 = moe_w1[li].astype(BF16), moe_w3[li].astype(BF16), moe_w2[li].astype(BF16)
            th = _tile(w1.shape[2], w1.shape[2] // 4)
            ys = expert_ffn(xs, tile_expert, tile_valid, w1, w3, w2, tm_e, th)
            tm_c = _tile(n, 512)
            n_lat_slots = 2 * xl.shape[0]
            xl = moe_combine(ys, slot_rows[:n_lat_slots], rw_l, xl, mods, lg2, lb2, tm_c, lat_row(tm_c))
            if need_ctx:
                xc = moe_combine(ys, slot_rows[n_lat_slots:], rw_c, xc, mods, lg2, lb2,
                                 _tile(n_ctx, 512), ctx_row)
    return xl.reshape(bsz, n, d)
```

```python
import functools
import math

import jax
import jax.numpy as jnp
from jax import lax
from jax.experimental import pallas as pl
from jax.experimental.pallas import tpu as pltpu

F32 = jnp.float32
BF16 = jnp.bfloat16
HIGHEST = lax.Precision.HIGHEST

D_MODEL = 1024
DEPTH = 4
GRID_W = 64
HEAD_DIM = 64
FOURIER_W = 512
FOURIER_GROUP_W = 128
ATT_HEADS = 8
ATT_KV_HEADS = 2
ATT_GROUP = 4
Q_W = 512
KV_W = 128
ATT_BLOCK = 128
ROPE_BASE = 10000.0
SSM_W = 512
SSM_GROUP = 16
SSM_GROUPS = 32
SSM_STATE = 64
GMLP_W = 512
GMLP_GROUPS = 4
CHUNK = 128
N_EXPERTS = 8
ALPHA = (2 * DEPTH) ** 0.25
LN_EPS = 1e-5
NEG_INF = -1e30
LOG2E = math.log2(math.e)
QK_SCALE = LOG2E * HEAD_DIM ** -0.5

LANE = 128
MOD_ROWS = 24
VMEM_LIMIT = 56 * 1024 * 1024


def _params(sem, disable_bounds_checks=False):
    return pltpu.CompilerParams(dimension_semantics=sem, vmem_limit_bytes=VMEM_LIMIT,
                                disable_bounds_checks=disable_bounds_checks)


def _silu(x):
    return x * jax.nn.sigmoid(x)


def _layer_norm(z, g, b):
    mu = jnp.mean(z, axis=-1, keepdims=True)
    d = z - mu
    var = jnp.mean(d * d, axis=-1, keepdims=True)
    return d * lax.rsqrt(var + LN_EPS) * g + b


def _mod_spec(part, row_fn):
    return pl.BlockSpec((None, None, 1, D_MODEL), lambda i, *_: (part, row_fn(i), 0, 0))


def _row_spec(tm, width):
    return pl.BlockSpec((tm, width), lambda i, *_: (i, 0))


def _full_spec(shape):
    nd = len(shape)
    return pl.BlockSpec(shape, lambda *_: (0,) * nd)


def _mod_kernel(c_ref, w_ref, b_ref, o_ref):
    s = _silu(c_ref[...])
    o_ref[...] = jnp.dot(s, w_ref[...], preferred_element_type=F32, precision=HIGHEST) + b_ref[...]


def modulation(cvec, w_mod, b_mod):
    depth = w_mod.shape[0]
    out = pl.pallas_call(
        _mod_kernel,
        grid=(depth, 6),
        in_specs=[
            pl.BlockSpec((MOD_ROWS, D_MODEL), lambda l, j: (0, 0)),
            pl.BlockSpec((None, D_MODEL, D_MODEL), lambda l, j: (l, 0, j)),
            pl.BlockSpec((None, None, 1, D_MODEL), lambda l, j: (l, j, 0, 0)),
        ],
        out_specs=pl.BlockSpec((None, None, MOD_ROWS, D_MODEL), lambda l, j: (l, j, 0, 0)),
        out_shape=jax.ShapeDtypeStruct((depth, 6, MOD_ROWS, D_MODEL), F32),
        compiler_params=_params(("arbitrary", "arbitrary")),
        name="modulation",
    )(cvec, w_mod, b_mod.reshape(depth, 6, 1, D_MODEL))
    return out.reshape(depth, 6, MOD_ROWS, 1, D_MODEL)


def _rope_slab(x, cos, sin_lo, sin_hi):
    return (x * cos + pltpu.roll(x, LANE - 16, 1) * sin_lo + pltpu.roll(x, 16, 1) * sin_hi)


def _even_in_kernel(rope, x_ref, sc_ref, sh_ref, w_ref, *refs):
    if rope:
        cos_ref, slo_ref, shi_ref, a_ref, q_ref, k_ref, v_ref = refs
    else:
        a_ref, q_ref, k_ref, v_ref = refs
    h = (x_ref[...] * (1.0 + sc_ref[...]) + sh_ref[...]).astype(BF16)
    p = jnp.dot(h, w_ref[...], preferred_element_type=F32)
    a_ref[...] = p[:, :FOURIER_W].astype(BF16)
    v_ref[...] = p[:, FOURIER_W + Q_W + KV_W:].astype(BF16)
    n_qk = (Q_W + KV_W) // LANE
    for s in range(n_qk):
        slab = p[:, FOURIER_W + s * LANE:FOURIER_W + (s + 1) * LANE]
        if rope:
            slab = _rope_slab(slab, cos_ref[...], slo_ref[...], shi_ref[...])
        if s < Q_W // LANE:
            q_ref[:, s * LANE:(s + 1) * LANE] = (slab * QK_SCALE).astype(BF16)
        else:
            k_ref[...] = slab.astype(BF16)


def even_in_proj(x, mods, w_in, rope_tabs, tm, tiles_per_batch, row_fn):
    t = x.shape[0]
    rope = rope_tabs is not None
    in_specs = [_row_spec(tm, D_MODEL), _mod_spec(1, row_fn), _mod_spec(0, row_fn),
                _full_spec(w_in.shape)]
    args = [x, mods, mods, w_in]
    if rope:
        tab_spec = pl.BlockSpec((tm, LANE), lambda i: (i % tiles_per_batch, 0))
        in_specs += [tab_spec] * 3
        args += list(rope_tabs)
    return pl.pallas_call(
        functools.partial(_even_in_kernel, rope),
        grid=(t // tm,),
        in_specs=in_specs,
        out_specs=[_row_spec(tm, FOURIER_W), _row_spec(tm, Q_W), _row_spec(tm, KV_W),
                   _row_spec(tm, KV_W)],
        out_shape=[jax.ShapeDtypeStruct((t, FOURIER_W), BF16), jax.ShapeDtypeStruct((t, Q_W), BF16),
                   jax.ShapeDtypeStruct((t, KV_W), BF16), jax.ShapeDtypeStruct((t, KV_W), BF16)],
        compiler_params=_params(("parallel",)),
        name="even_in_proj",
    )(*args)


def _fourier_kernel(n, tr, scale, a_ref, cs_ref, w_ref, o_ref, r_ref):
    j = pl.program_id(1)

    @pl.when(j == 0)
    def _():
        rows = min(n, 512)

        def body(c, carry):
            r0 = pl.multiple_of(c * rows, rows)
            blk = a_ref[pl.ds(r0, rows), :]
            for g in range(FOURIER_W // FOURIER_GROUP_W):
                acs = jnp.dot(blk[:, g * LANE:(g + 1) * LANE], cs_ref[...],
                              preferred_element_type=F32)
                r_ref[pl.ds(r0, rows), g * LANE:(g + 1) * LANE] = acs[:, :LANE].astype(BF16)
                r_ref[pl.ds(n + r0, rows), g * LANE:(g + 1) * LANE] = acs[:, LANE:].astype(BF16)
            return carry

        lax.fori_loop(0, n // rows, body, 0)

    y = jnp.dot(w_ref[...], r_ref[...], preferred_element_type=F32)
    o_ref[...] = (y * scale).astype(BF16)


def fourier_mix(a, n, cs, w, tr):
    bsz = a.shape[0] // n
    nt = n // tr
    scale = 1.0 / math.sqrt(n * FOURIER_GROUP_W)
    return pl.pallas_call(
        functools.partial(_fourier_kernel, n, tr, scale),
        grid=(bsz, nt),
        in_specs=[pl.BlockSpec((n, FOURIER_W), lambda b, j: (b, 0)),
                  pl.BlockSpec(cs.shape, lambda b, j: (0, 0)),
                  pl.BlockSpec((tr, 2 * n), lambda b, j: (j, 0))],
        out_specs=pl.BlockSpec((tr, FOURIER_W), lambda b, j: (b * nt + j, 0)),
        out_shape=jax.ShapeDtypeStruct(a.shape, BF16),
        scratch_shapes=[pltpu.VMEM((2 * n, FOURIER_W), BF16)],
        compiler_params=_params(("parallel", "arbitrary")),
        name="fourier_mix",
    )(a, cs, w)


def _gqa_heads(q_ref, sink_ref, kv_of, valid, o_ref):
    nq = q_ref.shape[0]
    outs = []
    for h in range(ATT_KV_HEADS):
        kb, vb = kv_of(h)
        heads = [h * ATT_GROUP + g for g in range(ATT_GROUP)]
        qh = jnp.concatenate([q_ref[:, hd * HEAD_DIM:(hd + 1) * HEAD_DIM] for hd in heads], axis=0)
        sink = jnp.concatenate([jnp.full((nq, 1), sink_ref[hd], F32) for hd in heads], axis=0)
        s = lax.dot_general(qh, kb, (((1,), (1,)), ((), ())), preferred_element_type=F32)
        if valid is not None:
            s = jnp.where(valid, s, NEG_INF)
        m = jnp.maximum(jnp.max(s, axis=-1, keepdims=True), sink)
        e = jnp.exp2(s - m)
        den = jnp.sum(e, axis=-1, keepdims=True) + jnp.exp2(sink - m)
        o = jnp.dot(e.astype(BF16), vb, preferred_element_type=F32) * (1.0 / den)
        outs += [o[g * nq:(g + 1) * nq] for g in range(ATT_GROUP)]
    o_ref[...] = jnp.concatenate(outs, axis=1).astype(BF16)


def _win_attn_kernel(n_blk, sink_ref, q_ref, kp_ref, kc_ref, kn_ref, vp_ref, vc_ref, vn_ref,
                     kx_ref, vx_ref, o_ref):
    blk = pl.program_id(1)
    n_ctx = kx_ref.shape[0]
    shape = (ATT_GROUP * ATT_BLOCK, 3 * ATT_BLOCK + n_ctx)
    qi = lax.broadcasted_iota(jnp.int32, shape, 0) & (ATT_BLOCK - 1)
    kj = lax.broadcasted_iota(jnp.int32, shape, 1)
    k_min = jnp.where(blk == 0, ATT_BLOCK, 0)
    k_max = jnp.where(blk == n_blk - 1, 2 * ATT_BLOCK, 3 * ATT_BLOCK)
    slack = jnp.minimum(ATT_BLOCK - jnp.abs(kj - ATT_BLOCK - qi), jnp.minimum(kj - k_min, k_max - 1 - kj))
    valid = jnp.where(kj >= 3 * ATT_BLOCK, 0, slack) >= 0

    def kv_of(h):
        cols = slice(h * HEAD_DIM, (h + 1) * HEAD_DIM)
        kb = jnp.concatenate([kp_ref[:, cols], kc_ref[:, cols], kn_ref[:, cols], kx_ref[:, cols]], axis=0)
        vb = jnp.concatenate([vp_ref[:, cols], vc_ref[:, cols], vn_ref[:, cols], vx_ref[:, cols]], axis=0)
        return kb, vb

    _gqa_heads(q_ref, sink_ref, kv_of, valid, o_ref)


def window_attention(q, k, v, k_ctx, v_ctx, sink, n, n_ctx):
    bsz = q.shape[0] // n
    n_blk = n // ATT_BLOCK

    def kv_spec(off):
        return pl.BlockSpec((ATT_BLOCK, KV_W),
                            lambda b, j: (b * n_blk + jnp.clip(j + off, 0, n_blk - 1), 0))

    ctx_spec = pl.BlockSpec((n_ctx, KV_W), lambda b, j: (b, 0))
    return pl.pallas_call(
        functools.partial(_win_attn_kernel, n_blk),
        grid=(bsz, n_blk),
        in_specs=[pl.BlockSpec(memory_space=pltpu.SMEM),
                  pl.BlockSpec((ATT_BLOCK, Q_W), lambda b, j: (b * n_blk + j, 0)),
                  kv_spec(-1), kv_spec(0), kv_spec(1), kv_spec(-1), kv_spec(0), kv_spec(1),
                  ctx_spec, ctx_spec],
        out_specs=pl.BlockSpec((ATT_BLOCK, Q_W), lambda b, j: (b * n_blk + j, 0)),
        out_shape=jax.ShapeDtypeStruct(q.shape, BF16),
        compiler_params=_params(("parallel", "arbitrary")),
        name="window_attention",
    )(sink, q, k, k, k, v, v, v, k_ctx, v_ctx)


def _ctx_attn_kernel(sink_ref, q_ref, k_ref, v_ref, o_ref):
    def kv_of(h):
        cols = slice(h * HEAD_DIM, (h + 1) * HEAD_DIM)
        return k_ref[:, cols], v_ref[:, cols]

    _gqa_heads(q_ref, sink_ref, kv_of, None, o_ref)


def context_attention(q, k, v, sink, n_ctx):
    bsz = q.shape[0] // n_ctx
    return pl.pallas_call(
        _ctx_attn_kernel,
        grid=(bsz,),
        in_specs=[pl.BlockSpec(memory_space=pltpu.SMEM), _row_spec(n_ctx, Q_W),
                  _row_spec(n_ctx, KV_W), _row_spec(n_ctx, KV_W)],
        out_specs=_row_spec(n_ctx, Q_W),
        out_shape=jax.ShapeDtypeStruct(q.shape, BF16),
        compiler_params=_params(("parallel",)),
        name="context_attention",
    )(sink, q, k, v)


def _even_out_kernel(f_ref, a_ref, x_ref, w_ref, g1_ref, sc_ref, sh_ref, lg_ref, lb_ref,
                     xo_ref, h_ref):
    half = w_ref.shape[0] // 2
    y = (jnp.dot(f_ref[...], w_ref[:half, :], preferred_element_type=F32)
         + jnp.dot(a_ref[...], w_ref[half:, :], preferred_element_type=F32))
    xn = _layer_norm(ALPHA * x_ref[...] + g1_ref[...] * y, lg_ref[...], lb_ref[...])
    xo_ref[...] = xn
    h_ref[...] = (xn * (1.0 + sc_ref[...]) + sh_ref[...]).astype(BF16)


def even_out_proj(f, a, x, w_out, mods, ln_g, ln_b, tm, row_fn):
    t = x.shape[0]
    vec = _full_spec((1, D_MODEL))
    return pl.pallas_call(
        _even_out_kernel,
        grid=(t // tm,),
        in_specs=[_row_spec(tm, FOURIER_W), _row_spec(tm, Q_W), _row_spec(tm, D_MODEL),
                  _full_spec(w_out.shape), _mod_spec(2, row_fn), _mod_spec(4, row_fn),
                  _mod_spec(3, row_fn), vec, vec],
        out_specs=[_row_spec(tm, D_MODEL), _row_spec(tm, D_MODEL)],
        out_shape=[jax.ShapeDtypeStruct((t, D_MODEL), F32), jax.ShapeDtypeStruct((t, D_MODEL), BF16)],
        compiler_params=_params(("parallel",)),
        name="even_out_proj",
    )(f, a, x, w_out, mods, mods, mods, ln_g, ln_b)


def _dense_ffn_kernel(h_ref, x_ref, w1_ref, w3_ref, w2_ref, g2_ref, lg_ref, lb_ref, o_ref, acc_ref):
    j = pl.program_id(1)

    @pl.when(j == 0)
    def _():
        acc_ref[...] = jnp.zeros_like(acc_ref)

    h = h_ref[...]
    he = _silu(jnp.dot(h, w1_ref[...], preferred_element_type=F32)) * jnp.dot(
        h, w3_ref[...], preferred_element_type=F32)
    acc_ref[...] += jnp.dot(he.astype(BF16), w2_ref[...], preferred_element_type=F32)

    @pl.when(j == pl.num_programs(1) - 1)
    def _():
        o_ref[...] = _layer_norm(ALPHA * x_ref[...] + g2_ref[...] * acc_ref[...], lg_ref[...], lb_ref[...])


def dense_ffn(h, x, w1, w3, w2, mods, ln_g, ln_b, tm, th, row_fn):
    t = x.shape[0]
    hidden = w1.shape[1]
    vec = pl.BlockSpec((1, D_MODEL), lambda i, j: (0, 0))
    return pl.pallas_call(
        _dense_ffn_kernel,
        grid=(t // tm, hidden // th),
        in_specs=[pl.BlockSpec((tm, D_MODEL), lambda i, j: (i, 0)),
                  pl.BlockSpec((tm, D_MODEL), lambda i, j: (i, 0)),
                  pl.BlockSpec((D_MODEL, th), lambda i, j: (0, j)),
                  pl.BlockSpec((D_MODEL, th), lambda i, j: (0, j)),
                  pl.BlockSpec((th, D_MODEL), lambda i, j: (j, 0)),
                  pl.BlockSpec((None, None, 1, D_MODEL), lambda i, j: (5, row_fn(i), 0, 0)),
                  vec, vec],
        out_specs=pl.BlockSpec((tm, D_MODEL), lambda i, j: (i, 0)),
        out_shape=jax.ShapeDtypeStruct((t, D_MODEL), F32),
        scratch_shapes=[pltpu.VMEM((tm, D_MODEL), F32)],
        compiler_params=_params(("parallel", "arbitrary")),
        name="dense_ffn",
    )(h, x, w1, w3, w2, mods, ln_g, ln_b)


def _odd_in_kernel(x_ref, sc_ref, sh_ref, wt_ref, wg_ref, ut_ref, gm_ref):
    h = (x_ref[...] * (1.0 + sc_ref[...]) + sh_ref[...]).astype(BF16)
    gm_ref[...] = jnp.dot(h, wg_ref[...], preferred_element_type=F32)
    for cc in range(ut_ref.shape[0]):
        hc = h[cc * CHUNK:(cc + 1) * CHUNK, :]
        ut = lax.dot_general(wt_ref[...], hc, (((1,), (1,)), ((), ())), preferred_element_type=F32)
        ut_ref[cc] = ut.astype(BF16)


def odd_in_proj(x, mods, w_ssm_t, w_gmlp, bsz, tm, row_fn):
    t = x.shape[0]
    n = t // bsz
    tiles_per_batch = n // tm
    cpt = tm // CHUNK
    return pl.pallas_call(
        _odd_in_kernel,
        grid=(t // tm,),
        in_specs=[_row_spec(tm, D_MODEL), _mod_spec(1, row_fn), _mod_spec(0, row_fn),
                  _full_spec(w_ssm_t.shape), _full_spec(w_gmlp.shape)],
        out_specs=[pl.BlockSpec((cpt, None, SSM_W, CHUNK),
                                lambda i: (i % tiles_per_batch, i // tiles_per_batch, 0, 0)),
                   _row_spec(tm, 2 * GMLP_W)],
        out_shape=[jax.ShapeDtypeStruct((n // CHUNK, bsz, SSM_W, CHUNK), BF16),
                   jax.ShapeDtypeStruct((t, 2 * GMLP_W), F32)],
        compiler_params=_params(("parallel",)),
        name="odd_in_proj",
    )(x, mods, mods, w_ssm_t, w_gmlp)


def _s5_kernel(n_cc, n_lc, bsz, uc_ref, ul_ref, kd_ref, g_ref, e_ref, a_ref, yc_ref, yl_ref, m_ref):
    n_ch = n_cc + n_lc
    for blk in range(SSM_GROUP * SSM_GROUP):
        s, tt = divmod(blk, SSM_GROUP)
        diag = jnp.broadcast_to(kd_ref[blk:blk + 1, :], (CHUNK, 2 * CHUNK))
        toe = pltpu.roll(diag, CHUNK + 1, 1, stride=1, stride_axis=0)
        m_ref[s * CHUNK:(s + 1) * CHUNK, tt * CHUNK:(tt + 1) * CHUNK] = toe[:, :CHUNK].astype(BF16)
    slabs = []
    for s in range(SSM_GROUP):
        sc = uc_ref[:, :, s, :].reshape(n_cc * bsz, CHUNK)
        sl = ul_ref[:, :, s, :].reshape(n_lc * bsz, CHUNK)
        slabs.append(jnp.concatenate([sc, sl], axis=0))
    lhs = jnp.concatenate(slabs, axis=1)
    y = jnp.dot(lhs, m_ref[...], preferred_element_type=F32)
    sm = jnp.dot(lhs, g_ref[...], preferred_element_type=F32)
    a = a_ref[...]
    n_st = 2 * SSM_STATE

    def step(state, a1, a2, inc):
        return a1 * state + a2 * pltpu.roll(state, SSM_STATE, 1) + inc

    state = jnp.zeros((bsz, n_st), F32)
    hf = []
    for k in range(n_ch):
        hf.append(state)
        state = step(state, a[0:1, :], a[1:2, :], sm[k * bsz:(k + 1) * bsz, :n_st])
    order = list(range(n_cc - 1, -1, -1)) + list(range(n_ch - 1, n_cc - 1, -1))
    state = jnp.zeros((bsz, n_st), F32)
    hr = [None] * n_ch
    for k in order:
        hr[k] = state
        state = step(state, a[2:3, :], a[3:4, :], sm[k * bsz:(k + 1) * bsz, n_st:])
    h_in = jnp.concatenate([jnp.concatenate(hf, axis=0), jnp.concatenate(hr, axis=0)], axis=1)
    y = y + jnp.dot(h_in.astype(BF16), e_ref[...], preferred_element_type=F32)
    rc = n_cc * bsz
    for s in range(SSM_GROUP):
        cols = slice(s * CHUNK, (s + 1) * CHUNK)
        yc_ref[:, :, s, :] = y[:rc, cols].reshape(n_cc, bsz, CHUNK)
        yl_ref[:, :, s, :] = y[rc:, cols].reshape(n_lc, bsz, CHUNK)


def s5_scan(ut_ctx, ut_lat, kd, g, e, a):
    n_cc, bsz = ut_ctx.shape[:2]
    n_lc = ut_lat.shape[0]

    def grp(nc):
        return pl.BlockSpec((nc, bsz, SSM_GROUP, CHUNK), lambda i: (0, 0, i, 0))

    def par(arr):
        return pl.BlockSpec((None,) + arr.shape[1:], lambda i: (i, 0, 0))

    return pl.pallas_call(
        functools.partial(_s5_kernel, n_cc, n_lc, bsz),
        grid=(SSM_GROUPS,),
        in_specs=[grp(n_cc), grp(n_lc), par(kd), par(g), par(e), par(a)],
        out_specs=[grp(n_cc), grp(n_lc)],
        out_shape=[jax.ShapeDtypeStruct(ut_ctx.shape, F32), jax.ShapeDtypeStruct(ut_lat.shape, F32)],
        scratch_shapes=[pltpu.VMEM((SSM_GROUP * CHUNK, SSM_GROUP * CHUNK), BF16)],
        compiler_params=_params(("parallel",)),
        name="s5_scan",
    )(ut_ctx, ut_lat, kd, g, e, a)


def s5_matrices(a_re, a_im, log_dt, b_re, b_im, c_re, c_im):
    t = CHUNK
    k_idx = jnp.arange(t + 1, dtype=F32)

    def one_dir(d):
        lam_re, lam_im = a_re[d].astype(F32), a_im[d].astype(F32)
        dt = jnp.exp(log_dt[d].astype(F32))[:, None]
        mag = jnp.exp(k_idx[:, None, None] * (lam_re * dt)[None])
        ang = k_idx[:, None, None] * (lam_im * dt)[None]
        pw_re, pw_im = mag * jnp.cos(ang), mag * jnp.sin(ang)
        nr, ni = pw_re[1] - 1.0, pw_im[1]
        den = lam_re * lam_re + lam_im * lam_im
        fr, fi = (nr * lam_re + ni * lam_im) / den, (ni * lam_re - nr * lam_im) / den
        br, bi = b_re[d].astype(F32), b_im[d].astype(F32)
        bb_re = fr[..., None] * br - fi[..., None] * bi
        bb_im = fr[..., None] * bi + fi[..., None] * br
        cr, ci = c_re[d].astype(F32), c_im[d].astype(F32)
        return pw_re, pw_im, bb_re, bb_im, cr, ci

    def conv_kernel(pw_re, pw_im, bb_re, bb_im, cr, ci):
        w_re = pw_re[:t, :, :, None] * bb_re[None] - pw_im[:t, :, :, None] * bb_im[None]
        w_im = pw_re[:t, :, :, None] * bb_im[None] + pw_im[:t, :, :, None] * bb_re[None]
        return (jnp.einsum('gtp,kgps->gkts', cr, w_re, precision=HIGHEST)
                - jnp.einsum('gtp,kgps->gkts', ci, w_im, precision=HIGHEST))

    fw, rv = one_dir(0), one_dir(1)
    kf, kr = conv_kernel(*fw), conv_kernel(*rv)
    kd = jnp.concatenate([kr[:, :0:-1], kf[:, :1] + kr[:, :1], kf[:, 1:],
                          jnp.zeros_like(kf[:, :1])], axis=1)
    kd = jnp.transpose(kd, (0, 3, 2, 1)).reshape(SSM_GROUPS, SSM_GROUP * SSM_GROUP, 2 * t)

    def summaries(pw_re, pw_im, bb_re, bb_im, cr, ci, reverse):
        pj_re = pw_re[:t] if reverse else pw_re[t - 1::-1][:t]
        pj_im = pw_im[:t] if reverse else pw_im[t - 1::-1][:t]
        g_re = jnp.einsum('jgp,gps->gsjp', pj_re, bb_re) - jnp.einsum('jgp,gps->gsjp', pj_im, bb_im)
        g_im = jnp.einsum('jgp,gps->gsjp', pj_re, bb_im) + jnp.einsum('jgp,gps->gsjp', pj_im, bb_re)
        pi_re = pw_re[t:0:-1] if reverse else pw_re[1:]
        pi_im = pw_im[t:0:-1] if reverse else pw_im[1:]
        e_re = jnp.einsum('gtp,igp->gpti', cr, pi_re) - jnp.einsum('gtp,igp->gpti', ci, pi_im)
        e_im = -(jnp.einsum('gtp,igp->gpti', cr, pi_im) + jnp.einsum('gtp,igp->gpti', ci, pi_re))
        a1 = jnp.concatenate([pw_re[t], pw_re[t]], axis=-1)
        a2 = jnp.concatenate([-pw_im[t], pw_im[t]], axis=-1)
        return g_re, g_im, e_re, e_im, a1, a2

    gf = summaries(*fw, False)
    gr = summaries(*rv, True)
    g = jnp.concatenate([gf[0], gf[1], gr[0], gr[1]], axis=-1)
    g = g.reshape(SSM_GROUPS, SSM_GROUP * t, 4 * SSM_STATE).astype(BF16)
    e = jnp.concatenate([gf[2], gf[3], gr[2], gr[3]], axis=1)
    e = e.reshape(SSM_GROUPS, 4 * SSM_STATE, SSM_GROUP * t).astype(BF16)
    a = jnp.stack([gf[4], gf[5], gr[4], gr[5]], axis=1)
    return kd, g, e, a


def _odd_out_kernel(yt_ref, ut_ref, gm_ref, x_ref, dsk_ref, wgt_ref, bgl_ref, ws_ref, bs_ref,
                    w_ref, wr_ref, cin_ref, g1_ref, sc_ref, sh_ref, lg_ref, lb_ref,
                    xo_ref, h_ref, ri_ref, cout_ref, mix_ref, cnt_ref):
    for cc in range(yt_ref.shape[0]):
        rows = slice(cc * CHUNK, (cc + 1) * CHUNK)
        t_t = dsk_ref[...] * ut_ref[cc].astype(F32) + yt_ref[cc]
        g_t = jax.nn.gelu(t_t)
        z_t = jnp.dot(wgt_ref[...], g_t.astype(BF16), preferred_element_type=F32) + bgl_ref[...]
        s_t = g_t * jax.nn.sigmoid(z_t)
        mix_ref[rows, :SSM_W] = s_t.T.astype(BF16)
        uv = jax.nn.gelu(gm_ref[rows, :])
        v = uv[:, GMLP_W:]
        mu = jnp.mean(v, axis=-1, keepdims=True)
        dv = v - mu
        var = jnp.mean(dv * dv, axis=-1, keepdims=True)
        v = (dv * lax.rsqrt(var + LN_EPS)).astype(BF16)
        for hh in range(GMLP_GROUPS):
            cols = slice(hh * LANE, (hh + 1) * LANE)
            vs = jnp.dot(ws_ref[hh], v[:, cols], preferred_element_type=F32) + bs_ref[hh]
            mix_ref[rows, SSM_W + hh * LANE:SSM_W + (hh + 1) * LANE] = (uv[:, cols] * vs).astype(BF16)
    y = jnp.dot(mix_ref[...], w_ref[...], preferred_element_type=F32)
    xn = _layer_norm(ALPHA * x_ref[...] + g1_ref[...] * y, lg_ref[...], lb_ref[...])
    xo_ref[...] = xn
    h2 = xn * (1.0 + sc_ref[...]) + sh_ref[...]
    h_ref[:, :D_MODEL] = h2
    h_hi = h2.astype(BF16)
    h_lo = (h2 - h_hi.astype(F32)).astype(BF16)
    logits = (jnp.dot(h_hi, wr_ref[0], preferred_element_type=F32)
              + jnp.dot(h_lo, wr_ref[0], preferred_element_type=F32)
              + jnp.dot(h_hi, wr_ref[1], preferred_element_type=F32))
    lane = lax.broadcasted_iota(jnp.int32, logits.shape, 1)
    lg = jnp.where(lane < N_EXPERTS, logits, -jnp.inf)
    m1 = jnp.max(lg, axis=-1, keepdims=True)
    i1 = jnp.min(jnp.where(lg == m1, lane, LANE), axis=-1, keepdims=True)
    lg2 = jnp.where(lane == i1, -jnp.inf, lg)
    m2 = jnp.max(lg2, axis=-1, keepdims=True)
    i2 = jnp.min(jnp.where(lg2 == m2, lane, LANE), axis=-1, keepdims=True)
    e2 = jnp.exp(m2 - m1)
    w1 = 1.0 / (1.0 + e2)
    w2 = e2 / (1.0 + e2)
    first_low = i1 < i2
    g_lo = jnp.where(first_low, w1, w2)
    g_hi = jnp.where(first_low, w2, w1)
    h_ref[:, D_MODEL:] = jnp.where(lane == 0, g_lo, jnp.where(lane == 1, g_hi, 0.0))
    pair = jnp.minimum(i1, i2) * N_EXPERTS + jnp.maximum(i1, i2)
    @pl.when(pl.program_id(0) == 0)
    def _():
        cnt_ref[...] = cin_ref[...]

    tm = logits.shape[0]
    onehot = jnp.where(lane == pair, 1.0, 0.0)
    before = jnp.where(lax.broadcasted_iota(jnp.int32, (tm, tm), 1)
                       < lax.broadcasted_iota(jnp.int32, (tm, tm), 0), 1.0, 0.0).astype(BF16)
    seen = jnp.dot(before, onehot.astype(BF16), preferred_element_type=F32) + cnt_ref[...]
    rank = jnp.sum(onehot * seen, axis=-1, keepdims=True).astype(jnp.int32)
    cnt_ref[...] += jnp.sum(onehot, axis=0, keepdims=True)
    cout_ref[...] = cnt_ref[...]
    ri_ref[...] = jnp.where(lane == 0, pair, jnp.where(lane == 1, rank, 0))


def odd_out_proj(yt, ut, gm, x, d_skip, w_glu_t, b_glu, w_s, b_s, w_out, w_router, pair_counts, mods,
                 ln_g, ln_b, bsz, tm, row_fn):
    t = x.shape[0]
    n = t // bsz
    tiles_per_batch = n // tm
    cpt = tm // CHUNK
    vec = _full_spec((1, D_MODEL))
    chunk_spec = pl.BlockSpec((cpt, None, SSM_W, CHUNK),
                              lambda i: (i % tiles_per_batch, i // tiles_per_batch, 0, 0))
    return pl.pallas_call(
        _odd_out_kernel,
        grid=(t // tm,),
        in_specs=[chunk_spec, chunk_spec, _row_spec(tm, 2 * GMLP_W), _row_spec(tm, D_MODEL),
                  _full_spec(d_skip.shape), _full_spec(w_glu_t.shape), _full_spec(b_glu.shape),
                  _full_spec(w_s.shape), _full_spec(b_s.shape), _full_spec(w_out.shape),
                  _full_spec(w_router.shape), _full_spec((1, LANE)), _mod_spec(2, row_fn),
                  _mod_spec(4, row_fn), _mod_spec(3, row_fn), vec, vec],
        out_specs=[_row_spec(tm, D_MODEL), _row_spec(tm, MOE_ROW_W), _row_spec(tm, LANE),
                   _full_spec((1, LANE))],
        out_shape=[jax.ShapeDtypeStruct((t, D_MODEL), F32), jax.ShapeDtypeStruct((t, MOE_ROW_W), F32),
                   jax.ShapeDtypeStruct((t, LANE), jnp.int32), jax.ShapeDtypeStruct((1, LANE), F32)],
        scratch_shapes=[pltpu.VMEM((tm, D_MODEL), BF16), pltpu.VMEM((1, LANE), F32)],
        compiler_params=_params(("arbitrary",)),
        name="odd_out_proj",
    )(yt, ut, gm, x, d_skip, w_glu_t, b_glu, w_s, b_s, w_out, w_router, pair_counts, mods, mods, mods,
      ln_g, ln_b)


N_PAIRS = N_EXPERTS * (N_EXPERTS - 1) // 2
MOE_ROW_W = D_MODEL + LANE
ROW_DMA_UNROLL = 8
ROW_WAIT_GROUP = 64


def _drain_rows(n_rows, src_row, dst_row, sem):
    def body(_, carry):
        for _ in range(ROW_WAIT_GROUP):
            pltpu.make_async_copy(src_row, dst_row, sem).wait()
        return carry

    lax.fori_loop(0, n_rows // ROW_WAIT_GROUP, body, 0)


def _dispatch_kernel(idx_ref, h_ref, xs_in_ref, xs_ref, sem):
    del xs_in_ref
    tm = h_ref.shape[0]

    def issue(r, carry):
        pltpu.make_async_copy(h_ref.at[pl.ds(r, 1), :],
                              xs_ref.at[pl.ds(idx_ref[0, 0, r], 1), :], sem).start()
        return carry

    lax.fori_loop(0, tm, issue, 0, unroll=ROW_DMA_UNROLL)
    _drain_rows(tm, h_ref.at[pl.ds(0, 1), :], xs_ref.at[pl.ds(0, 1), :], sem)


def moe_dispatch(h, dest, xs, tm):
    t, width = h.shape
    return pl.pallas_call(
        _dispatch_kernel,
        grid=(t // tm,),
        in_specs=[pl.BlockSpec((1, 1, tm), lambda i: (i, 0, 0), memory_space=pltpu.SMEM),
                  _row_spec(tm, width), pl.BlockSpec(memory_space=pl.ANY)],
        out_specs=pl.BlockSpec(memory_space=pl.ANY),
        out_shape=jax.ShapeDtypeStruct(xs.shape, xs.dtype),
        input_output_aliases={2: 0},
        scratch_shapes=[pltpu.SemaphoreType.DMA],
        compiler_params=_params(("arbitrary",), disable_bounds_checks=True),
        name="moe_dispatch",
    )(dest.reshape(t // tm, 1, tm), h, xs)


def _expert_ffn_kernel(te_ref, tv_ref, x_ref, w1_ref, w3_ref, w2_ref, o_ref):
    i = pl.program_id(0)
    k = pl.program_id(1)
    j = pl.program_id(2)

    @pl.when(jnp.logical_and(k == 0, j == 0))
    def _():
        o_ref[...] = jnp.zeros_like(o_ref)

    @pl.when(tv_ref[i] > 0)
    def _():
        h = x_ref[:, :D_MODEL].astype(BF16)
        gate = jnp.where(k == 0, x_ref[:, D_MODEL:D_MODEL + 1], x_ref[:, D_MODEL + 1:D_MODEL + 2])
        he = _silu(jnp.dot(h, w1_ref[...], preferred_element_type=F32)) * jnp.dot(
            h, w3_ref[...], preferred_element_type=F32)
        o_ref[...] += gate * jnp.dot(he.astype(BF16), w2_ref[...], preferred_element_type=F32)


def expert_ffn(xs, tile_experts, tile_valid, w1, w3, w2, tm, th):
    r = xs.shape[0]
    n_tiles = r // tm
    hidden = w1.shape[2]
    grid_spec = pltpu.PrefetchScalarGridSpec(
        num_scalar_prefetch=2,
        grid=(n_tiles, 2, hidden // th),
        in_specs=[pl.BlockSpec((tm, MOE_ROW_W), lambda i, k, j, te, tv: (i, 0)),
                  pl.BlockSpec((None, D_MODEL, th), lambda i, k, j, te, tv: (te[k * n_tiles + i], 0, j)),
                  pl.BlockSpec((None, D_MODEL, th), lambda i, k, j, te, tv: (te[k * n_tiles + i], 0, j)),
                  pl.BlockSpec((None, th, D_MODEL), lambda i, k, j, te, tv: (te[k * n_tiles + i], j, 0))],
        out_specs=pl.BlockSpec((tm, D_MODEL), lambda i, k, j, te, tv: (i, 0)),
    )
    return pl.pallas_call(
        _expert_ffn_kernel,
        grid_spec=grid_spec,
        out_shape=jax.ShapeDtypeStruct((r, D_MODEL), F32),
        compiler_params=_params(("arbitrary", "arbitrary", "arbitrary")),
        name="expert_ffn",
    )(tile_experts, tile_valid, xs, w1, w3, w2)


def _combine_kernel(idx_ref, ys_ref, x_ref, g2_ref, lg_ref, lb_ref, o_ref, buf_ref, sem):
    tm = o_ref.shape[0]

    def issue(r, carry):
        pltpu.make_async_copy(ys_ref.at[pl.ds(idx_ref[0, 0, r], 1), :],
                              buf_ref.at[pl.ds(r, 1), :], sem).start()
        return carry

    lax.fori_loop(0, tm, issue, 0, unroll=ROW_DMA_UNROLL)
    _drain_rows(tm, ys_ref.at[pl.ds(0, 1), :], buf_ref.at[pl.ds(0, 1), :], sem)
    o_ref[...] = _layer_norm(ALPHA * x_ref[...] + g2_ref[...] * buf_ref[...], lg_ref[...], lb_ref[...])


def moe_combine(ys, dest, x, mods, ln_g, ln_b, tm, row_fn):
    t = x.shape[0]
    vec = _full_spec((1, D_MODEL))
    return pl.pallas_call(
        _combine_kernel,
        grid=(t // tm,),
        in_specs=[pl.BlockSpec((1, 1, tm), lambda i: (i, 0, 0), memory_space=pltpu.SMEM),
                  pl.BlockSpec(memory_space=pl.ANY), _row_spec(tm, D_MODEL),
                  _mod_spec(5, row_fn), vec, vec],
        out_specs=_row_spec(tm, D_MODEL),
        out_shape=jax.ShapeDtypeStruct((t, D_MODEL), F32),
        scratch_shapes=[pltpu.VMEM((tm, D_MODEL), F32), pltpu.SemaphoreType.DMA],
        compiler_params=_params(("arbitrary",), disable_bounds_checks=True),
        name="moe_combine",
    )(dest.reshape(t // tm, 1, tm), ys, x, mods, ln_g, ln_b)


def route_plan(pair_counts, n_tokens, tm):
    n_tiles = n_tokens // tm + N_PAIRS
    counts = pair_counts.astype(jnp.int32)
    tiles_per = (counts + tm - 1) // tm
    tile_end = jnp.cumsum(tiles_per)
    row_start = (tile_end - tiles_per) * tm
    tile_ids = jnp.arange(n_tiles)
    tile_valid = (tile_ids < tile_end[-1]).astype(jnp.int32)
    last = jnp.maximum(tile_end[-1] - 1, 0)
    tile_pair = jnp.searchsorted(tile_end, jnp.minimum(tile_ids, last), side='right').astype(jnp.int32)
    tile_pair = jnp.minimum(tile_pair, N_EXPERTS * N_EXPERTS - 1)
    tile_experts = jnp.concatenate([tile_pair // N_EXPERTS, tile_pair % N_EXPERTS])
    return row_start, tile_experts, tile_valid, n_tiles


def pair_rows(route_i, row_start):
    pair, rank = route_i[:, 0], route_i[:, 1]
    onehot = pair[:, None] == jnp.arange(N_EXPERTS * N_EXPERTS)[None, :]
    return rank + jnp.sum(jnp.where(onehot, row_start[None, :], 0), axis=1)


def _rope_tables(n):
    pos = jnp.arange(n, dtype=jnp.int32)
    rows, cols = (pos // GRID_W).astype(F32), (pos % GRID_W).astype(F32)
    quarter = HEAD_DIM // 4
    d = jnp.arange(HEAD_DIM)
    inv = ROPE_BASE ** (-(d % quarter).astype(F32) / quarter)
    ang = jnp.where(d[None, :] < HEAD_DIM // 2, rows[:, None], cols[:, None]) * inv[None, :]
    cos, sin = jnp.cos(ang), jnp.sin(ang)
    low = (d % (2 * quarter) < quarter)[None, :]
    tabs = (cos, jnp.where(low, -sin, 0.0), jnp.where(low, 0.0, sin))
    return tuple(jnp.tile(t, (1, LANE // HEAD_DIM)) for t in tabs)


def _dft_tables(n):
    idx = jnp.arange(n, dtype=jnp.int32)
    ang = ((idx[:, None] * idx[None, :]) % n).astype(F32) * (2.0 * math.pi / n)
    return jnp.cos(ang), jnp.sin(ang)


def _tile(total, want):
    return want if total % want == 0 else total


def kernel(x, c, ctx, c_ctx, w_mod, b_mod, ln_g, ln_b, ev_w_in, ev_w_out, ev_sink, ev_w1, ev_w3, ev_w2,
           od_w_in, ssm_a_re, ssm_a_im, ssm_log_dt, ssm_b_re, ssm_b_im, ssm_c_re, ssm_c_im, ssm_d,
           ssm_w_glu, ssm_b_glu, gmlp_w_s, gmlp_b_s, od_w_out, moe_w_router, moe_w1, moe_w3, moe_w2):
    bsz, n, d = x.shape
    n_ctx = ctx.shape[1]
    depth = w_mod.shape[0]
    assert d == D_MODEL and bsz + 1 <= MOD_ROWS and n % CHUNK == 0 and n_ctx % CHUNK == 0
    xl = x.reshape(bsz * n, d)
    xc = ctx.reshape(bsz * n_ctx, d)
    cvec = jnp.zeros((MOD_ROWS, d), F32).at[:bsz].set(c).at[bsz].set(c_ctx)
    mods_all = modulation(cvec, w_mod, b_mod)

    tm_l = _tile(n, 1024)
    tm_f = _tile(n, 512)

    def lat_row(tm):
        return lambda i: i // (n // tm)

    def ctx_row(i):
        return bsz

    rope_tabs = _rope_tables(n)
    cc_c, sc_c = _dft_tables(FOURIER_GROUP_W)
    cs = jnp.concatenate([cc_c, sc_c], axis=1).astype(BF16)
    cn, sn = _dft_tables(n)
    w_n = jnp.concatenate([cn, -sn], axis=1).astype(BF16)
    cx, sx = _dft_tables(n_ctx)
    w_x = jnp.concatenate([cx, -sx], axis=1).astype(BF16)

    for layer in range(depth):
        need_ctx = layer < depth - 1
        li = layer // 2
        mods = mods_all[layer]
        lg1, lb1 = ln_g[layer, 0][None, :], ln_b[layer, 0][None, :]
        lg2, lb2 = ln_g[layer, 1][None, :], ln_b[layer, 1][None, :]
        if layer % 2 == 0:
            w_in = ev_w_in[li].astype(BF16)
            w_out = ev_w_out[li].astype(BF16)
            sink = ev_sink[li].astype(F32) * LOG2E
            a_l, q_l, k_l, v_l = even_in_proj(xl, mods, w_in, rope_tabs, tm_l, n // tm_l, lat_row(tm_l))
            a_c, q_c, k_c, v_c = even_in_proj(xc, mods, w_in, None, n_ctx, 1, ctx_row)
            f_l = fourier_mix(a_l, n, cs, w_n, tm_f)
            at_l = window_attention(q_l, k_l, v_l, k_c, v_c, sink, n, n_ctx)
            xl, h_l = even_out_proj(f_l, at_l, xl, w_out, mods, lg1, lb1, tm_l, lat_row(tm_l))
            w1, w3, w2 = ev_w1[li].astype(BF16), ev_w3[li].astype(BF16), ev_w2[li].astype(BF16)
            th = _tile(w1.shape[1], w1.shape[1] // 2)
            xl = dense_ffn(h_l, xl, w1, w3, w2, mods, lg2, lb2, tm_f, th, lat_row(tm_f))
            if need_ctx:
                f_c = fourier_mix(a_c, n_ctx, cs, w_x, n_ctx)
                at_c = context_attention(q_c, k_c, v_c, sink, n_ctx)
                xc, h_c = even_out_proj(f_c, at_c, xc, w_out, mods, lg1, lb1, n_ctx, ctx_row)
                xc = dense_ffn(h_c, xc, w1, w3, w2, mods, lg2, lb2, n_ctx, th, ctx_row)
        else:
            w_in = od_w_in[li]
            w_ssm_t = w_in[:, :SSM_W].T.astype(BF16)
            w_gmlp = w_in[:, SSM_W:].astype(BF16)
            ut_l, gm_l = odd_in_proj(xl, mods, w_ssm_t, w_gmlp, bsz, tm_l, lat_row(tm_l))
            ut_c, gm_c = odd_in_proj(xc, mods, w_ssm_t, w_gmlp, bsz, n_ctx, ctx_row)
            m, g, e, a = s5_matrices(ssm_a_re[li], ssm_a_im[li], ssm_log_dt[li], ssm_b_re[li],
                                     ssm_b_im[li], ssm_c_re[li], ssm_c_im[li])
            yt_c, yt_l = s5_scan(ut_c, ut_l, m, g, e, a)
            w_r = jnp.pad(moe_w_router[li].astype(F32), ((0, 0), (0, LANE - N_EXPERTS)))
            w_r_hi = w_r.astype(BF16)
            w_r_split = jnp.stack([w_r_hi, (w_r - w_r_hi.astype(F32)).astype(BF16)])
            post = (ssm_d[li].astype(F32)[:, None], ssm_w_glu[li].T.astype(BF16),
                    ssm_b_glu[li].astype(F32)[:, None], gmlp_w_s[li].astype(BF16),
                    gmlp_b_s[li].astype(F32)[:, :, None], od_w_out[li].astype(BF16), w_r_split)
            counts = jnp.zeros((1, LANE), F32)
            xl, h_l, ri_l, counts = odd_out_proj(yt_l, ut_l, gm_l, xl, *post, counts, mods, lg1, lb1, bsz,
                                                 tm_l, lat_row(tm_l))
            n_tok = xl.shape[0]
            if need_ctx:
                xc, h_c, ri_c, counts = odd_out_proj(yt_c, ut_c, gm_c, xc, *post, counts, mods, lg1, lb1,
                                                     bsz, n_ctx, ctx_row)
                n_tok += xc.shape[0]
            tm_e = _tile(n_tok, 512)
            tm_d = _tile(n, 512)
            row_start, tile_experts, tile_valid, n_tiles = route_plan(
                counts[0, :N_EXPERTS * N_EXPERTS], n_tok, tm_e)
            dest_l = pair_rows(ri_l, row_start)
            xs = moe_dispatch(h_l, dest_l, jnp.zeros((n_tiles * tm_e, MOE_ROW_W), F32), tm_d)
            if need_ctx:
                dest_c = pair_rows(ri_c, row_start)
                xs = moe_dispatch(h_c, dest_c, xs, _tile(n_ctx, 512))
            w1, w3, w2 = moe_w1[li].astype(BF16), moe_w3[li].astype(BF16), moe_w2[li].astype(BF16)
            th = _tile(w1.shape[2], w1.shape[2] // 2)
            ys = expert_ffn(xs, tile_experts, tile_valid, w1, w3, w2, tm_e, th)
            xl = moe_combine(ys, dest_l, xl, mods, lg2, lb2, tm_d, lat_row(tm_d))
            if need_ctx:
                xc = moe_combine(ys, dest_c, xc, mods, lg2, lb2, _tile(n_ctx, 512), ctx_row)
    return xl.reshape(bsz, n, d)
```

```python
import functools
import math

import jax
import jax.numpy as jnp
from jax import lax
from jax.experimental import pallas as pl
from jax.experimental.pallas import tpu as pltpu

F32 = jnp.float32
BF16 = jnp.bfloat16
HIGHEST = lax.Precision.HIGHEST

D_MODEL = 1024
DEPTH = 4
GRID_W = 64
HEAD_DIM = 64
FOURIER_W = 512
FOURIER_GROUP_W = 128
ATT_HEADS = 8
ATT_KV_HEADS = 2
ATT_GROUP = 4
Q_W = 512
KV_W = 128
ATT_BLOCK = 128
ROPE_BASE = 10000.0
SSM_W = 512
SSM_GROUP = 16
SSM_GROUPS = 32
SSM_STATE = 64
GMLP_W = 512
GMLP_GROUPS = 4
CHUNK = 128
N_EXPERTS = 8
ALPHA = (2 * DEPTH) ** 0.25
LN_EPS = 1e-5
NEG_INF = -1e30
LOG2E = math.log2(math.e)
QK_SCALE = LOG2E * HEAD_DIM ** -0.5

LANE = 128
MOD_ROWS = 24
VMEM_LIMIT = 56 * 1024 * 1024


def _params(sem, disable_bounds_checks=False):
    return pltpu.CompilerParams(dimension_semantics=sem, vmem_limit_bytes=VMEM_LIMIT,
                                disable_bounds_checks=disable_bounds_checks)


def _silu(x):
    return x * jax.nn.sigmoid(x)


def _layer_norm(z, g, b):
    mu = jnp.mean(z, axis=-1, keepdims=True)
    d = z - mu
    var = jnp.mean(d * d, axis=-1, keepdims=True)
    return d * lax.rsqrt(var + LN_EPS) * g + b


def _mod_spec(part, row_fn):
    return pl.BlockSpec((None, None, 1, D_MODEL), lambda i, *_: (part, row_fn(i), 0, 0))


def _row_spec(tm, width):
    return pl.BlockSpec((tm, width), lambda i, *_: (i, 0))


def _full_spec(shape):
    nd = len(shape)
    return pl.BlockSpec(shape, lambda *_: (0,) * nd)


def _mod_kernel(c_ref, w_ref, b_ref, o_ref):
    s = _silu(c_ref[...])
    o_ref[...] = jnp.dot(s, w_ref[...], preferred_element_type=F32, precision=HIGHEST) + b_ref[...]


def modulation(cvec, w_mod, b_mod):
    depth = w_mod.shape[0]
    out = pl.pallas_call(
        _mod_kernel,
        grid=(depth, 6),
        in_specs=[
            pl.BlockSpec((MOD_ROWS, D_MODEL), lambda l, j: (0, 0)),
            pl.BlockSpec((None, D_MODEL, D_MODEL), lambda l, j: (l, 0, j)),
            pl.BlockSpec((None, None, 1, D_MODEL), lambda l, j: (l, j, 0, 0)),
        ],
        out_specs=pl.BlockSpec((None, None, MOD_ROWS, D_MODEL), lambda l, j: (l, j, 0, 0)),
        out_shape=jax.ShapeDtypeStruct((depth, 6, MOD_ROWS, D_MODEL), F32),
        compiler_params=_params(("arbitrary", "arbitrary")),
        name="modulation",
    )(cvec, w_mod, b_mod.reshape(depth, 6, 1, D_MODEL))
    return out.reshape(depth, 6, MOD_ROWS, 1, D_MODEL)


def _rope_slab(x, cos, sin_lo, sin_hi):
    return (x * cos + pltpu.roll(x, LANE - 16, 1) * sin_lo + pltpu.roll(x, 16, 1) * sin_hi)


def _even_in_kernel(rope, x_ref, sc_ref, sh_ref, w_ref, *refs):
    if rope:
        cos_ref, slo_ref, shi_ref, a_ref, q_ref, k_ref, v_ref = refs
    else:
        a_ref, q_ref, k_ref, v_ref = refs
    h = (x_ref[...] * (1.0 + sc_ref[...]) + sh_ref[...]).astype(BF16)
    p = jnp.dot(h, w_ref[...], preferred_element_type=F32)
    a_ref[...] = p[:, :FOURIER_W].astype(BF16)
    v_ref[...] = p[:, FOURIER_W + Q_W + KV_W:].astype(BF16)
    n_qk = (Q_W + KV_W) // LANE
    for s in range(n_qk):
        slab = p[:, FOURIER_W + s * LANE:FOURIER_W + (s + 1) * LANE]
        if rope:
            slab = _rope_slab(slab, cos_ref[...], slo_ref[...], shi_ref[...])
        if s < Q_W // LANE:
            q_ref[:, s * LANE:(s + 1) * LANE] = (slab * QK_SCALE).astype(BF16)
        else:
            k_ref[...] = slab.astype(BF16)


def even_in_proj(x, mods, w_in, rope_tabs, tm, tiles_per_batch, row_fn):
    t = x.shape[0]
    rope = rope_tabs is not None
    in_specs = [_row_spec(tm, D_MODEL), _mod_spec(1, row_fn), _mod_spec(0, row_fn),
                _full_spec(w_in.shape)]
    args = [x, mods, mods, w_in]
    if rope:
        tab_spec = pl.BlockSpec((tm, LANE), lambda i: (i % tiles_per_batch, 0))
        in_specs += [tab_spec] * 3
        args += list(rope_tabs)
    return pl.pallas_call(
        functools.partial(_even_in_kernel, rope),
        grid=(t // tm,),
        in_specs=in_specs,
        out_specs=[_row_spec(tm, FOURIER_W), _row_spec(tm, Q_W), _row_spec(tm, KV_W),
                   _row_spec(tm, KV_W)],
        out_shape=[jax.ShapeDtypeStruct((t, FOURIER_W), BF16), jax.ShapeDtypeStruct((t, Q_W), BF16),
                   jax.ShapeDtypeStruct((t, KV_W), BF16), jax.ShapeDtypeStruct((t, KV_W), BF16)],
        compiler_params=_params(("parallel",)),
        name="even_in_proj",
    )(*args)


def _fourier_kernel(n, tr, scale, a_ref, cs_ref, w_ref, o_ref, r_ref):
    j = pl.program_id(1)

    @pl.when(j == 0)
    def _():
        rows = min(n, 512)

        def body(c, carry):
            r0 = pl.multiple_of(c * rows, rows)
            blk = a_ref[pl.ds(r0, rows), :]
            for g in range(FOURIER_W // FOURIER_GROUP_W):
                acs = jnp.dot(blk[:, g * LANE:(g + 1) * LANE], cs_ref[...],
                              preferred_element_type=F32)
                r_ref[pl.ds(r0, rows), g * LANE:(g + 1) * LANE] = acs[:, :LANE].astype(BF16)
                r_ref[pl.ds(n + r0, rows), g * LANE:(g + 1) * LANE] = acs[:, LANE:].astype(BF16)
            return carry

        lax.fori_loop(0, n // rows, body, 0)

    y = jnp.dot(w_ref[...], r_ref[...], preferred_element_type=F32)
    o_ref[...] = (y * scale).astype(BF16)


def fourier_mix(a, n, cs, w, tr):
    bsz = a.shape[0] // n
    nt = n // tr
    scale = 1.0 / math.sqrt(n * FOURIER_GROUP_W)
    return pl.pallas_call(
        functools.partial(_fourier_kernel, n, tr, scale),
        grid=(bsz, nt),
        in_specs=[pl.BlockSpec((n, FOURIER_W), lambda b, j: (b, 0)),
                  pl.BlockSpec(cs.shape, lambda b, j: (0, 0)),
                  pl.BlockSpec((tr, 2 * n), lambda b, j: (j, 0))],
        out_specs=pl.BlockSpec((tr, FOURIER_W), lambda b, j: (b * nt + j, 0)),
        out_shape=jax.ShapeDtypeStruct(a.shape, BF16),
        scratch_shapes=[pltpu.VMEM((2 * n, FOURIER_W), BF16)],
        compiler_params=_params(("parallel", "arbitrary")),
        name="fourier_mix",
    )(a, cs, w)


def _gqa_heads(q_ref, sink_ref, kv_of, valid, o_ref):
    nq = q_ref.shape[0]
    outs = []
    for h in range(ATT_KV_HEADS):
        kb, vb = kv_of(h)
        heads = [h * ATT_GROUP + g for g in range(ATT_GROUP)]
        qh = jnp.concatenate([q_ref[:, hd * HEAD_DIM:(hd + 1) * HEAD_DIM] for hd in heads], axis=0)
        sink = jnp.concatenate([jnp.full((nq, 1), sink_ref[hd], F32) for hd in heads], axis=0)
        s = lax.dot_general(qh, kb, (((1,), (1,)), ((), ())), preferred_element_type=F32)
        if valid is not None:
            v_prev, v_next = valid
            s = jnp.concatenate([jnp.where(v_prev, s[:, :ATT_BLOCK], NEG_INF),
                                 s[:, ATT_BLOCK:2 * ATT_BLOCK],
                                 jnp.where(v_next, s[:, 2 * ATT_BLOCK:3 * ATT_BLOCK], NEG_INF),
                                 s[:, 3 * ATT_BLOCK:]], axis=1)
        m = jnp.maximum(jnp.max(s, axis=-1, keepdims=True), sink)
        e = jnp.exp2(s - m)
        den = jnp.sum(e, axis=-1, keepdims=True) + jnp.exp2(sink - m)
        o = jnp.dot(e.astype(BF16), vb, preferred_element_type=F32) * (1.0 / den)
        outs += [o[g * nq:(g + 1) * nq] for g in range(ATT_GROUP)]
    o_ref[...] = jnp.concatenate(outs, axis=1).astype(BF16)


def _win_attn_kernel(n_blk, sink_ref, q_ref, kp_ref, kc_ref, kn_ref, vp_ref, vc_ref, vn_ref,
                     kx_ref, vx_ref, o_ref):
    blk = pl.program_id(1)
    shape = (ATT_GROUP * ATT_BLOCK, ATT_BLOCK)
    qi = lax.broadcasted_iota(jnp.int32, shape, 0) & (ATT_BLOCK - 1)
    kj = lax.broadcasted_iota(jnp.int32, shape, 1)
    k_min = jnp.where(blk == 0, ATT_BLOCK, 0)
    k_lim = jnp.where(blk == n_blk - 1, -1, ATT_BLOCK - 1)
    valid = (kj >= jnp.maximum(qi, k_min), kj <= jnp.minimum(qi, k_lim))

    def kv_of(h):
        cols = slice(h * HEAD_DIM, (h + 1) * HEAD_DIM)
        kb = jnp.concatenate([kp_ref[:, cols], kc_ref[:, cols], kn_ref[:, cols], kx_ref[:, cols]], axis=0)
        vb = jnp.concatenate([vp_ref[:, cols], vc_ref[:, cols], vn_ref[:, cols], vx_ref[:, cols]], axis=0)
        return kb, vb

    _gqa_heads(q_ref, sink_ref, kv_of, valid, o_ref)


def window_attention(q, k, v, k_ctx, v_ctx, sink, n, n_ctx):
    bsz = q.shape[0] // n
    n_blk = n // ATT_BLOCK

    def kv_spec(off):
        return pl.BlockSpec((ATT_BLOCK, KV_W),
                            lambda b, j: (b * n_blk + jnp.clip(j + off, 0, n_blk - 1), 0))

    ctx_spec = pl.BlockSpec((n_ctx, KV_W), lambda b, j: (b, 0))
    return pl.pallas_call(
        functools.partial(_win_attn_kernel, n_blk),
        grid=(bsz, n_blk),
        in_specs=[pl.BlockSpec(memory_space=pltpu.SMEM),
                  pl.BlockSpec((ATT_BLOCK, Q_W), lambda b, j: (b * n_blk + j, 0)),
                  kv_spec(-1), kv_spec(0), kv_spec(1), kv_spec(-1), kv_spec(0), kv_spec(1),
                  ctx_spec, ctx_spec],
        out_specs=pl.BlockSpec((ATT_BLOCK, Q_W), lambda b, j: (b * n_blk + j, 0)),
        out_shape=jax.ShapeDtypeStruct(q.shape, BF16),
        compiler_params=_params(("parallel", "arbitrary")),
        name="window_attention",
    )(sink, q, k, k, k, v, v, v, k_ctx, v_ctx)


def _ctx_attn_kernel(sink_ref, q_ref, k_ref, v_ref, o_ref):
    def kv_of(h):
        cols = slice(h * HEAD_DIM, (h + 1) * HEAD_DIM)
        return k_ref[:, cols], v_ref[:, cols]

    _gqa_heads(q_ref, sink_ref, kv_of, None, o_ref)


def context_attention(q, k, v, sink, n_ctx):
    bsz = q.shape[0] // n_ctx
    return pl.pallas_call(
        _ctx_attn_kernel,
        grid=(bsz,),
        in_specs=[pl.BlockSpec(memory_space=pltpu.SMEM), _row_spec(n_ctx, Q_W),
                  _row_spec(n_ctx, KV_W), _row_spec(n_ctx, KV_W)],
        out_specs=_row_spec(n_ctx, Q_W),
        out_shape=jax.ShapeDtypeStruct(q.shape, BF16),
        compiler_params=_params(("parallel",)),
        name="context_attention",
    )(sink, q, k, v)


def _even_out_kernel(f_ref, a_ref, x_ref, w_ref, g1_ref, sc_ref, sh_ref, lg_ref, lb_ref,
                     xo_ref, h_ref):
    half = w_ref.shape[0] // 2
    y = (jnp.dot(f_ref[...], w_ref[:half, :], preferred_element_type=F32)
         + jnp.dot(a_ref[...], w_ref[half:, :], preferred_element_type=F32))
    xn = _layer_norm(ALPHA * x_ref[...] + g1_ref[...] * y, lg_ref[...], lb_ref[...])
    xo_ref[...] = xn
    h_ref[...] = (xn * (1.0 + sc_ref[...]) + sh_ref[...]).astype(BF16)


def even_out_proj(f, a, x, w_out, mods, ln_g, ln_b, tm, row_fn):
    t = x.shape[0]
    vec = _full_spec((1, D_MODEL))
    return pl.pallas_call(
        _even_out_kernel,
        grid=(t // tm,),
        in_specs=[_row_spec(tm, FOURIER_W), _row_spec(tm, Q_W), _row_spec(tm, D_MODEL),
                  _full_spec(w_out.shape), _mod_spec(2, row_fn), _mod_spec(4, row_fn),
                  _mod_spec(3, row_fn), vec, vec],
        out_specs=[_row_spec(tm, D_MODEL), _row_spec(tm, D_MODEL)],
        out_shape=[jax.ShapeDtypeStruct((t, D_MODEL), F32), jax.ShapeDtypeStruct((t, D_MODEL), BF16)],
        compiler_params=_params(("parallel",)),
        name="even_out_proj",
    )(f, a, x, w_out, mods, mods, mods, ln_g, ln_b)


def _dense_ffn_kernel(h_ref, x_ref, w1_ref, w3_ref, w2_ref, g2_ref, lg_ref, lb_ref, o_ref, acc_ref):
    j = pl.program_id(1)

    @pl.when(j == 0)
    def _():
        acc_ref[...] = jnp.zeros_like(acc_ref)

    h = h_ref[...]
    he = _silu(jnp.dot(h, w1_ref[...], preferred_element_type=F32)) * jnp.dot(
        h, w3_ref[...], preferred_element_type=F32)
    acc_ref[...] += jnp.dot(he.astype(BF16), w2_ref[...], preferred_element_type=F32)

    @pl.when(j == pl.num_programs(1) - 1)
    def _():
        o_ref[...] = _layer_norm(ALPHA * x_ref[...] + g2_ref[...] * acc_ref[...], lg_ref[...], lb_ref[...])


def dense_ffn(h, x, w1, w3, w2, mods, ln_g, ln_b, tm, th, row_fn):
    t = x.shape[0]
    hidden = w1.shape[1]
    vec = pl.BlockSpec((1, D_MODEL), lambda i, j: (0, 0))
    return pl.pallas_call(
        _dense_ffn_kernel,
        grid=(t // tm, hidden // th),
        in_specs=[pl.BlockSpec((tm, D_MODEL), lambda i, j: (i, 0)),
                  pl.BlockSpec((tm, D_MODEL), lambda i, j: (i, 0)),
                  pl.BlockSpec((D_MODEL, th), lambda i, j: (0, j)),
                  pl.BlockSpec((D_MODEL, th), lambda i, j: (0, j)),
                  pl.BlockSpec((th, D_MODEL), lambda i, j: (j, 0)),
                  pl.BlockSpec((None, None, 1, D_MODEL), lambda i, j: (5, row_fn(i), 0, 0)),
                  vec, vec],
        out_specs=pl.BlockSpec((tm, D_MODEL), lambda i, j: (i, 0)),
        out_shape=jax.ShapeDtypeStruct((t, D_MODEL), F32),
        scratch_shapes=[pltpu.VMEM((tm, D_MODEL), F32)],
        compiler_params=_params(("parallel", "arbitrary")),
        name="dense_ffn",
    )(h, x, w1, w3, w2, mods, ln_g, ln_b)


def _odd_in_kernel(x_ref, sc_ref, sh_ref, wt_ref, wg_ref, ut_ref, gm_ref):
    h = (x_ref[...] * (1.0 + sc_ref[...]) + sh_ref[...]).astype(BF16)
    gm_ref[...] = jnp.dot(h, wg_ref[...], preferred_element_type=F32)
    for cc in range(ut_ref.shape[0]):
        hc = h[cc * CHUNK:(cc + 1) * CHUNK, :]
        ut = lax.dot_general(wt_ref[...], hc, (((1,), (1,)), ((), ())), preferred_element_type=F32)
        ut_ref[cc] = ut.astype(BF16)


def odd_in_proj(x, mods, w_ssm_t, w_gmlp, bsz, tm, row_fn):
    t = x.shape[0]
    n = t // bsz
    tiles_per_batch = n // tm
    cpt = tm // CHUNK
    return pl.pallas_call(
        _odd_in_kernel,
        grid=(t // tm,),
        in_specs=[_row_spec(tm, D_MODEL), _mod_spec(1, row_fn), _mod_spec(0, row_fn),
                  _full_spec(w_ssm_t.shape), _full_spec(w_gmlp.shape)],
        out_specs=[pl.BlockSpec((cpt, None, SSM_W, CHUNK),
                                lambda i: (i % tiles_per_batch, i // tiles_per_batch, 0, 0)),
                   _row_spec(tm, 2 * GMLP_W)],
        out_shape=[jax.ShapeDtypeStruct((n // CHUNK, bsz, SSM_W, CHUNK), BF16),
                   jax.ShapeDtypeStruct((t, 2 * GMLP_W), F32)],
        compiler_params=_params(("parallel",)),
        name="odd_in_proj",
    )(x, mods, mods, w_ssm_t, w_gmlp)


def _s5_kernel(n_cc, n_lc, bsz, uc_ref, ul_ref, kd_ref, g_ref, e_ref, a_ref, yc_ref, yl_ref, m_ref):
    n_ch = n_cc + n_lc
    for blk in range(SSM_GROUP * SSM_GROUP):
        s, tt = divmod(blk, SSM_GROUP)
        diag = jnp.broadcast_to(kd_ref[blk:blk + 1, :], (CHUNK, 2 * CHUNK))
        toe = pltpu.roll(diag, CHUNK + 1, 1, stride=1, stride_axis=0)
        m_ref[s * CHUNK:(s + 1) * CHUNK, tt * CHUNK:(tt + 1) * CHUNK] = toe[:, :CHUNK].astype(BF16)
    slabs = []
    for s in range(SSM_GROUP):
        sc = uc_ref[:, :, s, :].reshape(n_cc * bsz, CHUNK)
        sl = ul_ref[:, :, s, :].reshape(n_lc * bsz, CHUNK)
        slabs.append(jnp.concatenate([sc, sl], axis=0))
    lhs = jnp.concatenate(slabs, axis=1)
    y = jnp.dot(lhs, m_ref[...], preferred_element_type=F32)
    sm = jnp.dot(lhs, g_ref[...], preferred_element_type=F32)
    a = a_ref[...]
    n_st = 2 * SSM_STATE

    def step(state, a1, a2, inc):
        return a1 * state + a2 * pltpu.roll(state, SSM_STATE, 1) + inc

    state = jnp.zeros((bsz, n_st), F32)
    hf = []
    for k in range(n_ch):
        hf.append(state)
        state = step(state, a[0:1, :], a[1:2, :], sm[k * bsz:(k + 1) * bsz, :n_st])
    order = list(range(n_cc - 1, -1, -1)) + list(range(n_ch - 1, n_cc - 1, -1))
    state = jnp.zeros((bsz, n_st), F32)
    hr = [None] * n_ch
    for k in order:
        hr[k] = state
        state = step(state, a[2:3, :], a[3:4, :], sm[k * bsz:(k + 1) * bsz, n_st:])
    h_in = jnp.concatenate([jnp.concatenate(hf, axis=0), jnp.concatenate(hr, axis=0)], axis=1)
    y = y + jnp.dot(h_in.astype(BF16), e_ref[...], preferred_element_type=F32)
    rc = n_cc * bsz
    for s in range(SSM_GROUP):
        cols = slice(s * CHUNK, (s + 1) * CHUNK)
        yc_ref[:, :, s, :] = y[:rc, cols].reshape(n_cc, bsz, CHUNK)
        yl_ref[:, :, s, :] = y[rc:, cols].reshape(n_lc, bsz, CHUNK)


def s5_scan(ut_ctx, ut_lat, kd, g, e, a):
    n_cc, bsz = ut_ctx.shape[:2]
    n_lc = ut_lat.shape[0]

    def grp(nc):
        return pl.BlockSpec((nc, bsz, SSM_GROUP, CHUNK), lambda i: (0, 0, i, 0))

    def par(arr):
        return pl.BlockSpec((None,) + arr.shape[1:], lambda i: (i, 0, 0))

    return pl.pallas_call(
        functools.partial(_s5_kernel, n_cc, n_lc, bsz),
        grid=(SSM_GROUPS,),
        in_specs=[grp(n_cc), grp(n_lc), par(kd), par(g), par(e), par(a)],
        out_specs=[grp(n_cc), grp(n_lc)],
        out_shape=[jax.ShapeDtypeStruct(ut_ctx.shape, F32), jax.ShapeDtypeStruct(ut_lat.shape, F32)],
        scratch_shapes=[pltpu.VMEM((SSM_GROUP * CHUNK, SSM_GROUP * CHUNK), BF16)],
        compiler_params=_params(("parallel",)),
        name="s5_scan",
    )(ut_ctx, ut_lat, kd, g, e, a)


def s5_matrices(a_re, a_im, log_dt, b_re, b_im, c_re, c_im):
    t = CHUNK
    k_idx = jnp.arange(t + 1, dtype=F32)

    def one_dir(d):
        lam_re, lam_im = a_re[d].astype(F32), a_im[d].astype(F32)
        dt = jnp.exp(log_dt[d].astype(F32))[:, None]
        mag = jnp.exp(k_idx[:, None, None] * (lam_re * dt)[None])
        ang = k_idx[:, None, None] * (lam_im * dt)[None]
        pw_re, pw_im = mag * jnp.cos(ang), mag * jnp.sin(ang)
        nr, ni = pw_re[1] - 1.0, pw_im[1]
        den = lam_re * lam_re + lam_im * lam_im
        fr, fi = (nr * lam_re + ni * lam_im) / den, (ni * lam_re - nr * lam_im) / den
        br, bi = b_re[d].astype(F32), b_im[d].astype(F32)
        bb_re = fr[..., None] * br - fi[..., None] * bi
        bb_im = fr[..., None] * bi + fi[..., None] * br
        cr, ci = c_re[d].astype(F32), c_im[d].astype(F32)
        return pw_re, pw_im, bb_re, bb_im, cr, ci

    def conv_kernel(pw_re, pw_im, bb_re, bb_im, cr, ci):
        w_re = pw_re[:t, :, :, None] * bb_re[None] - pw_im[:t, :, :, None] * bb_im[None]
        w_im = pw_re[:t, :, :, None] * bb_im[None] + pw_im[:t, :, :, None] * bb_re[None]
        return (jnp.einsum('gtp,kgps->gkts', cr, w_re, precision=HIGHEST)
                - jnp.einsum('gtp,kgps->gkts', ci, w_im, precision=HIGHEST))

    fw, rv = one_dir(0), one_dir(1)
    kf, kr = conv_kernel(*fw), conv_kernel(*rv)
    kd = jnp.concatenate([kr[:, :0:-1], kf[:, :1] + kr[:, :1], kf[:, 1:],
                          jnp.zeros_like(kf[:, :1])], axis=1)
    kd = jnp.transpose(kd, (0, 3, 2, 1)).reshape(SSM_GROUPS, SSM_GROUP * SSM_GROUP, 2 * t)

    def summaries(pw_re, pw_im, bb_re, bb_im, cr, ci, reverse):
        pj_re = pw_re[:t] if reverse else pw_re[t - 1::-1][:t]
        pj_im = pw_im[:t] if reverse else pw_im[t - 1::-1][:t]
        g_re = jnp.einsum('jgp,gps->gsjp', pj_re, bb_re) - jnp.einsum('jgp,gps->gsjp', pj_im, bb_im)
        g_im = jnp.einsum('jgp,gps->gsjp', pj_re, bb_im) + jnp.einsum('jgp,gps->gsjp', pj_im, bb_re)
        pi_re = pw_re[t:0:-1] if reverse else pw_re[1:]
        pi_im = pw_im[t:0:-1] if reverse else pw_im[1:]
        e_re = jnp.einsum('gtp,igp->gpti', cr, pi_re) - jnp.einsum('gtp,igp->gpti', ci, pi_im)
        e_im = -(jnp.einsum('gtp,igp->gpti', cr, pi_im) + jnp.einsum('gtp,igp->gpti', ci, pi_re))
        a1 = jnp.concatenate([pw_re[t], pw_re[t]], axis=-1)
        a2 = jnp.concatenate([-pw_im[t], pw_im[t]], axis=-1)
        return g_re, g_im, e_re, e_im, a1, a2

    gf = summaries(*fw, False)
    gr = summaries(*rv, True)
    g = jnp.concatenate([gf[0], gf[1], gr[0], gr[1]], axis=-1)
    g = g.reshape(SSM_GROUPS, SSM_GROUP * t, 4 * SSM_STATE).astype(BF16)
    e = jnp.concatenate([gf[2], gf[3], gr[2], gr[3]], axis=1)
    e = e.reshape(SSM_GROUPS, 4 * SSM_STATE, SSM_GROUP * t).astype(BF16)
    a = jnp.stack([gf[4], gf[5], gr[4], gr[5]], axis=1)
    return kd, g, e, a


def _odd_out_kernel(yt_ref, ut_ref, gm_ref, x_ref, dsk_ref, wgt_ref, bgl_ref, ws_ref, bs_ref,
                    w_ref, wr_ref, cin_ref, g1_ref, sc_ref, sh_ref, lg_ref, lb_ref,
                    xo_ref, h_ref, ri_ref, cout_ref, mix_ref, cnt_ref):
    for cc in range(yt_ref.shape[0]):
        rows = slice(cc * CHUNK, (cc + 1) * CHUNK)
        t_t = dsk_ref[...] * ut_ref[cc].astype(F32) + yt_ref[cc]
        g_t = jax.nn.gelu(t_t)
        z_t = jnp.dot(wgt_ref[...], g_t.astype(BF16), preferred_element_type=F32) + bgl_ref[...]
        s_t = g_t * jax.nn.sigmoid(z_t)
        mix_ref[rows, :SSM_W] = s_t.T.astype(BF16)
        uv = jax.nn.gelu(gm_ref[rows, :])
        v = uv[:, GMLP_W:]
        mu = jnp.mean(v, axis=-1, keepdims=True)
        dv = v - mu
        var = jnp.mean(dv * dv, axis=-1, keepdims=True)
        v = (dv * lax.rsqrt(var + LN_EPS)).astype(BF16)
        for hh in range(GMLP_GROUPS):
            cols = slice(hh * LANE, (hh + 1) * LANE)
            vs = jnp.dot(ws_ref[hh], v[:, cols], preferred_element_type=F32) + bs_ref[hh]
            mix_ref[rows, SSM_W + hh * LANE:SSM_W + (hh + 1) * LANE] = (uv[:, cols] * vs).astype(BF16)
    y = jnp.dot(mix_ref[...], w_ref[...], preferred_element_type=F32)
    xn = _layer_norm(ALPHA * x_ref[...] + g1_ref[...] * y, lg_ref[...], lb_ref[...])
    xo_ref[...] = xn
    h2 = xn * (1.0 + sc_ref[...]) + sh_ref[...]
    h_ref[:, :D_MODEL] = h2
    h_hi = h2.astype(BF16)
    h_lo = (h2 - h_hi.astype(F32)).astype(BF16)
    logits = (jnp.dot(h_hi, wr_ref[0], preferred_element_type=F32)
              + jnp.dot(h_lo, wr_ref[0], preferred_element_type=F32)
              + jnp.dot(h_hi, wr_ref[1], preferred_element_type=F32))
    lane = lax.broadcasted_iota(jnp.int32, logits.shape, 1)
    lg = jnp.where(lane < N_EXPERTS, logits, -jnp.inf)
    m1 = jnp.max(lg, axis=-1, keepdims=True)
    i1 = jnp.min(jnp.where(lg == m1, lane, LANE), axis=-1, keepdims=True)
    lg2 = jnp.where(lane == i1, -jnp.inf, lg)
    m2 = jnp.max(lg2, axis=-1, keepdims=True)
    i2 = jnp.min(jnp.where(lg2 == m2, lane, LANE), axis=-1, keepdims=True)
    e2 = jnp.exp(m2 - m1)
    w1 = 1.0 / (1.0 + e2)
    w2 = e2 / (1.0 + e2)
    first_low = i1 < i2
    g_lo = jnp.where(first_low, w1, w2)
    g_hi = jnp.where(first_low, w2, w1)
    h_ref[:, D_MODEL:] = jnp.where(lane == 0, g_lo, jnp.where(lane == 1, g_hi, 0.0))
    pair = jnp.minimum(i1, i2) * N_EXPERTS + jnp.maximum(i1, i2)
    @pl.when(pl.program_id(0) == 0)
    def _():
        cnt_ref[...] = cin_ref[...]

    tm = logits.shape[0]
    onehot = jnp.where(lane == pair, 1.0, 0.0)
    before = jnp.where(lax.broadcasted_iota(jnp.int32, (tm, tm), 1)
                       < lax.broadcasted_iota(jnp.int32, (tm, tm), 0), 1.0, 0.0).astype(BF16)
    seen = jnp.dot(before, onehot.astype(BF16), preferred_element_type=F32) + cnt_ref[...]
    rank = jnp.sum(onehot * seen, axis=-1, keepdims=True).astype(jnp.int32)
    cnt_ref[...] += jnp.sum(onehot, axis=0, keepdims=True)
    cout_ref[...] = cnt_ref[...]
    ri_ref[...] = jnp.where(lane == 0, pair, jnp.where(lane == 1, rank, 0))


def odd_out_proj(yt, ut, gm, x, d_skip, w_glu_t, b_glu, w_s, b_s, w_out, w_router, pair_counts, mods,
                 ln_g, ln_b, bsz, tm, row_fn):
    t = x.shape[0]
    n = t // bsz
    tiles_per_batch = n // tm
    cpt = tm // CHUNK
    vec = _full_spec((1, D_MODEL))
    chunk_spec = pl.BlockSpec((cpt, None, SSM_W, CHUNK),
                              lambda i: (i % tiles_per_batch, i // tiles_per_batch, 0, 0))
    return pl.pallas_call(
        _odd_out_kernel,
        grid=(t // tm,),
        in_specs=[chunk_spec, chunk_spec, _row_spec(tm, 2 * GMLP_W), _row_spec(tm, D_MODEL),
                  _full_spec(d_skip.shape), _full_spec(w_glu_t.shape), _full_spec(b_glu.shape),
                  _full_spec(w_s.shape), _full_spec(b_s.shape), _full_spec(w_out.shape),
                  _full_spec(w_router.shape), _full_spec((1, LANE)), _mod_spec(2, row_fn),
                  _mod_spec(4, row_fn), _mod_spec(3, row_fn), vec, vec],
        out_specs=[_row_spec(tm, D_MODEL), _row_spec(tm, MOE_ROW_W), _row_spec(tm, LANE),
                   _full_spec((1, LANE))],
        out_shape=[jax.ShapeDtypeStruct((t, D_MODEL), F32), jax.ShapeDtypeStruct((t, MOE_ROW_W), F32),
                   jax.ShapeDtypeStruct((t, LANE), jnp.int32), jax.ShapeDtypeStruct((1, LANE), F32)],
        scratch_shapes=[pltpu.VMEM((tm, D_MODEL), BF16), pltpu.VMEM((1, LANE), F32)],
        compiler_params=_params(("arbitrary",)),
        name="odd_out_proj",
    )(yt, ut, gm, x, d_skip, w_glu_t, b_glu, w_s, b_s, w_out, w_router, pair_counts, mods, mods, mods,
      ln_g, ln_b)


N_PAIRS = N_EXPERTS * (N_EXPERTS - 1) // 2
MOE_ROW_W = D_MODEL + LANE
ROW_DMA_UNROLL = 8
ROW_WAIT_GROUP = 64


def _drain_rows(n_rows, src_row, dst_row, sem):
    def body(_, carry):
        for _ in range(ROW_WAIT_GROUP):
            pltpu.make_async_copy(src_row, dst_row, sem).wait()
        return carry

    lax.fori_loop(0, n_rows // ROW_WAIT_GROUP, body, 0)


def _dispatch_kernel(idx_ref, h_ref, xs_in_ref, xs_ref, sem):
    del xs_in_ref
    tm = h_ref.shape[0]

    def issue(r, carry):
        pltpu.make_async_copy(h_ref.at[pl.ds(r, 1), :],
                              xs_ref.at[pl.ds(idx_ref[0, 0, r], 1), :], sem).start()
        return carry

    lax.fori_loop(0, tm, issue, 0, unroll=ROW_DMA_UNROLL)
    _drain_rows(tm, h_ref.at[pl.ds(0, 1), :], xs_ref.at[pl.ds(0, 1), :], sem)


def moe_dispatch(h, dest, xs, tm):
    t, width = h.shape
    return pl.pallas_call(
        _dispatch_kernel,
        grid=(t // tm,),
        in_specs=[pl.BlockSpec((1, 1, tm), lambda i: (i, 0, 0), memory_space=pltpu.SMEM),
                  _row_spec(tm, width), pl.BlockSpec(memory_space=pl.ANY)],
        out_specs=pl.BlockSpec(memory_space=pl.ANY),
        out_shape=jax.ShapeDtypeStruct(xs.shape, xs.dtype),
        input_output_aliases={2: 0},
        scratch_shapes=[pltpu.SemaphoreType.DMA],
        compiler_params=_params(("arbitrary",), disable_bounds_checks=True),
        name="moe_dispatch",
    )(dest.reshape(t // tm, 1, tm), h, xs)


def _expert_ffn_kernel(te_ref, tv_ref, x_ref, w1_ref, w3_ref, w2_ref, o_ref):
    i = pl.program_id(0)
    k = pl.program_id(1)
    j = pl.program_id(2)

    @pl.when(jnp.logical_and(k == 0, j == 0))
    def _():
        o_ref[...] = jnp.zeros_like(o_ref)

    @pl.when(tv_ref[i] > 0)
    def _():
        h = x_ref[:, :D_MODEL].astype(BF16)
        gate = jnp.where(k == 0, x_ref[:, D_MODEL:D_MODEL + 1], x_ref[:, D_MODEL + 1:D_MODEL + 2])
        he = _silu(jnp.dot(h, w1_ref[...], preferred_element_type=F32)) * jnp.dot(
            h, w3_ref[...], preferred_element_type=F32)
        o_ref[...] += gate * jnp.dot(he.astype(BF16), w2_ref[...], preferred_element_type=F32)


def expert_ffn(xs, tile_experts, tile_valid, w1, w3, w2, tm, th):
    r = xs.shape[0]
    n_tiles = r // tm
    hidden = w1.shape[2]
    grid_spec = pltpu.PrefetchScalarGridSpec(
        num_scalar_prefetch=2,
        grid=(n_tiles, 2, hidden // th),
        in_specs=[pl.BlockSpec((tm, MOE_ROW_W), lambda i, k, j, te, tv: (i, 0)),
                  pl.BlockSpec((None, D_MODEL, th), lambda i, k, j, te, tv: (te[k * n_tiles + i], 0, j * tv[i])),
                  pl.BlockSpec((None, D_MODEL, th), lambda i, k, j, te, tv: (te[k * n_tiles + i], 0, j * tv[i])),
                  pl.BlockSpec((None, th, D_MODEL), lambda i, k, j, te, tv: (te[k * n_tiles + i], j * tv[i], 0))],
        out_specs=pl.BlockSpec((tm, D_MODEL), lambda i, k, j, te, tv: (i, 0)),
    )
    return pl.pallas_call(
        _expert_ffn_kernel,
        grid_spec=grid_spec,
        out_shape=jax.ShapeDtypeStruct((r, D_MODEL), F32),
        compiler_params=_params(("arbitrary", "arbitrary", "arbitrary")),
        name="expert_ffn",
    )(tile_experts, tile_valid, xs, w1, w3, w2)


def _combine_kernel(idx_ref, ys_ref, x_ref, g2_ref, lg_ref, lb_ref, o_ref, buf_ref, sem):
    tm = o_ref.shape[0]

    def issue(r, carry):
        pltpu.make_async_copy(ys_ref.at[pl.ds(idx_ref[0, 0, r], 1), :],
                              buf_ref.at[pl.ds(r, 1), :], sem).start()
        return carry

    lax.fori_loop(0, tm, issue, 0, unroll=ROW_DMA_UNROLL)
    _drain_rows(tm, ys_ref.at[pl.ds(0, 1), :], buf_ref.at[pl.ds(0, 1), :], sem)
    o_ref[...] = _layer_norm(ALPHA * x_ref[...] + g2_ref[...] * buf_ref[...], lg_ref[...], lb_ref[...])


def moe_combine(ys, dest, x, mods, ln_g, ln_b, tm, row_fn):
    t = x.shape[0]
    vec = _full_spec((1, D_MODEL))
    return pl.pallas_call(
        _combine_kernel,
        grid=(t // tm,),
        in_specs=[pl.BlockSpec((1, 1, tm), lambda i: (i, 0, 0), memory_space=pltpu.SMEM),
                  pl.BlockSpec(memory_space=pl.ANY), _row_spec(tm, D_MODEL),
                  _mod_spec(5, row_fn), vec, vec],
        out_specs=_row_spec(tm, D_MODEL),
        out_shape=jax.ShapeDtypeStruct((t, D_MODEL), F32),
        scratch_shapes=[pltpu.VMEM((tm, D_MODEL), F32), pltpu.SemaphoreType.DMA],
        compiler_params=_params(("arbitrary",), disable_bounds_checks=True),
        name="moe_combine",
    )(dest.reshape(t // tm, 1, tm), ys, x, mods, ln_g, ln_b)


def route_plan(pair_counts, n_tokens, tm):
    n_tiles = n_tokens // tm + N_PAIRS
    counts = pair_counts.astype(jnp.int32)
    tiles_per = (counts + tm - 1) // tm
    tile_end = jnp.cumsum(tiles_per)
    row_start = (tile_end - tiles_per) * tm
    tile_ids = jnp.arange(n_tiles)
    tile_valid = (tile_ids < tile_end[-1]).astype(jnp.int32)
    last = jnp.maximum(tile_end[-1] - 1, 0)
    tile_pair = jnp.searchsorted(tile_end, jnp.minimum(tile_ids, last), side='right').astype(jnp.int32)
    tile_pair = jnp.minimum(tile_pair, N_EXPERTS * N_EXPERTS - 1)
    hi = tile_pair % N_EXPERTS
    tile_experts = jnp.concatenate([jnp.where(tile_valid > 0, tile_pair // N_EXPERTS, hi), hi])
    return row_start, tile_experts, tile_valid, n_tiles


def pair_rows(route_i, row_start):
    pair, rank = route_i[:, 0], route_i[:, 1]
    onehot = pair[:, None] == jnp.arange(N_EXPERTS * N_EXPERTS)[None, :]
    return rank + jnp.sum(jnp.where(onehot, row_start[None, :], 0), axis=1)


def _rope_tables(n):
    pos = jnp.arange(n, dtype=jnp.int32)
    rows, cols = (pos // GRID_W).astype(F32), (pos % GRID_W).astype(F32)
    quarter = HEAD_DIM // 4
    d = jnp.arange(HEAD_DIM)
    inv = ROPE_BASE ** (-(d % quarter).astype(F32) / quarter)
    ang = jnp.where(d[None, :] < HEAD_DIM // 2, rows[:, None], cols[:, None]) * inv[None, :]
    cos, sin = jnp.cos(ang), jnp.sin(ang)
    low = (d % (2 * quarter) < quarter)[None, :]
    tabs = (cos, jnp.where(low, -sin, 0.0), jnp.where(low, 0.0, sin))
    return tuple(jnp.tile(t, (1, LANE // HEAD_DIM)) for t in tabs)


def _dft_tables(n):
    idx = jnp.arange(n, dtype=jnp.int32)

    def table(rows, period):
        ang = ((rows[:, None] * idx[None, :]) % period).astype(F32) * (2.0 * math.pi / period)
        return jnp.cos(ang), jnp.sin(ang)

    if n <= GRID_W or n % GRID_W:
        return table(idx, n)
    m = n // GRID_W
    ca, sa = table(jnp.arange(m, dtype=jnp.int32), m)
    cb, sb = table(jnp.arange(GRID_W, dtype=jnp.int32), n)
    cos = ca[:, None, :] * cb[None, :, :] - sa[:, None, :] * sb[None, :, :]
    sin = sa[:, None, :] * cb[None, :, :] + ca[:, None, :] * sb[None, :, :]
    return cos.reshape(n, n), sin.reshape(n, n)


def _tile(total, want):
    return want if total % want == 0 else total


def kernel(x, c, ctx, c_ctx, w_mod, b_mod, ln_g, ln_b, ev_w_in, ev_w_out, ev_sink, ev_w1, ev_w3, ev_w2,
           od_w_in, ssm_a_re, ssm_a_im, ssm_log_dt, ssm_b_re, ssm_b_im, ssm_c_re, ssm_c_im, ssm_d,
           ssm_w_glu, ssm_b_glu, gmlp_w_s, gmlp_b_s, od_w_out, moe_w_router, moe_w1, moe_w3, moe_w2):
    bsz, n, d = x.shape
    n_ctx = ctx.shape[1]
    depth = w_mod.shape[0]
    assert d == D_MODEL and bsz + 1 <= MOD_ROWS and n % CHUNK == 0 and n_ctx % CHUNK == 0
    xl = x.reshape(bsz * n, d)
    xc = ctx.reshape(bsz * n_ctx, d)
    cvec = jnp.zeros((MOD_ROWS, d), F32).at[:bsz].set(c).at[bsz].set(c_ctx)
    mods_all = modulation(cvec, w_mod, b_mod)

    tm_l = _tile(n, 1024)
    tm_f = _tile(n, 512)

    def lat_row(tm):
        return lambda i: i // (n // tm)

    def ctx_row(i):
        return bsz

    rope_tabs = _rope_tables(n)
    cc_c, sc_c = _dft_tables(FOURIER_GROUP_W)
    cs = jnp.concatenate([cc_c, sc_c], axis=1).astype(BF16)
    cn, sn = _dft_tables(n)
    w_n = jnp.concatenate([cn, -sn], axis=1).astype(BF16)
    cx, sx = _dft_tables(n_ctx)
    w_x = jnp.concatenate([cx, -sx], axis=1).astype(BF16)

    for layer in range(depth):
        need_ctx = layer < depth - 1
        li = layer // 2
        mods = mods_all[layer]
        lg1, lb1 = ln_g[layer, 0][None, :], ln_b[layer, 0][None, :]
        lg2, lb2 = ln_g[layer, 1][None, :], ln_b[layer, 1][None, :]
        if layer % 2 == 0:
            w_in = ev_w_in[li].astype(BF16)
            w_out = ev_w_out[li].astype(BF16)
            sink = ev_sink[li].astype(F32) * LOG2E
            a_l, q_l, k_l, v_l = even_in_proj(xl, mods, w_in, rope_tabs, tm_l, n // tm_l, lat_row(tm_l))
            a_c, q_c, k_c, v_c = even_in_proj(xc, mods, w_in, None, n_ctx, 1, ctx_row)
            f_l = fourier_mix(a_l, n, cs, w_n, tm_f)
            at_l = window_attention(q_l, k_l, v_l, k_c, v_c, sink, n, n_ctx)
            xl, h_l = even_out_proj(f_l, at_l, xl, w_out, mods, lg1, lb1, tm_l, lat_row(tm_l))
            w1, w3, w2 = ev_w1[li].astype(BF16), ev_w3[li].astype(BF16), ev_w2[li].astype(BF16)
            th = _tile(w1.shape[1], w1.shape[1] // 2)
            xl = dense_ffn(h_l, xl, w1, w3, w2, mods, lg2, lb2, tm_f, th, lat_row(tm_f))
            if need_ctx:
                f_c = fourier_mix(a_c, n_ctx, cs, w_x, n_ctx)
                at_c = context_attention(q_c, k_c, v_c, sink, n_ctx)
                xc, h_c = even_out_proj(f_c, at_c, xc, w_out, mods, lg1, lb1, n_ctx, ctx_row)
                xc = dense_ffn(h_c, xc, w1, w3, w2, mods, lg2, lb2, n_ctx, th, ctx_row)
        else:
            w_in = od_w_in[li]
            w_ssm_t = w_in[:, :SSM_W].T.astype(BF16)
            w_gmlp = w_in[:, SSM_W:].astype(BF16)
            ut_l, gm_l = odd_in_proj(xl, mods, w_ssm_t, w_gmlp, bsz, tm_l, lat_row(tm_l))
            ut_c, gm_c = odd_in_proj(xc, mods, w_ssm_t, w_gmlp, bsz, n_ctx, ctx_row)
            m, g, e, a = s5_matrices(ssm_a_re[li], ssm_a_im[li], ssm_log_dt[li], ssm_b_re[li],
                                     ssm_b_im[li], ssm_c_re[li], ssm_c_im[li])
            yt_c, yt_l = s5_scan(ut_c, ut_l, m, g, e, a)
            w_r = jnp.pad(moe_w_router[li].astype(F32), ((0, 0), (0, LANE - N_EXPERTS)))
            w_r_hi = w_r.astype(BF16)
            w_r_split = jnp.stack([w_r_hi, (w_r - w_r_hi.astype(F32)).astype(BF16)])
            post = (ssm_d[li].astype(F32)[:, None], ssm_w_glu[li].T.astype(BF16),
                    ssm_b_glu[li].astype(F32)[:, None], gmlp_w_s[li].astype(BF16),
                    gmlp_b_s[li].astype(F32)[:, :, None], od_w_out[li].astype(BF16), w_r_split)
            counts = jnp.zeros((1, LANE), F32)
            xl, h_l, ri_l, counts = odd_out_proj(yt_l, ut_l, gm_l, xl, *post, counts, mods, lg1, lb1, bsz,
                                                 tm_l, lat_row(tm_l))
            n_tok = xl.shape[0]
            if need_ctx:
                xc, h_c, ri_c, counts = odd_out_proj(yt_c, ut_c, gm_c, xc, *post, counts, mods, lg1, lb1,
                                                     bsz, n_ctx, ctx_row)
                n_tok += xc.shape[0]
            tm_e = _tile(n_tok, 512)
            tm_d = _tile(n, 512)
            row_start, tile_experts, tile_valid, n_tiles = route_plan(
                counts[0, :N_EXPERTS * N_EXPERTS], n_tok, tm_e)
            dest_l = pair_rows(ri_l, row_start)
            xs = moe_dispatch(h_l, dest_l, jnp.zeros((n_tiles * tm_e, MOE_ROW_W), F32), tm_d)
            if need_ctx:
                dest_c = pair_rows(ri_c, row_start)
                xs = moe_dispatch(h_c, dest_c, xs, _tile(n_ctx, 512))
            w1, w3, w2 = moe_w1[li].astype(BF16), moe_w3[li].astype(BF16), moe_w2[li].astype(BF16)
            th = _tile(w1.shape[2], w1.shape[2] // 2)
            ys = expert_ffn(xs, tile_experts, tile_valid, w1, w3, w2, tm_e, th)
            xl = moe_combine(ys, dest_l, xl, mods, lg2, lb2, tm_d, lat_row(tm_d))
            if need_ctx:
                xc = moe_combine(ys, dest_c, xc, mods, lg2, lb2, _tile(n_ctx, 512), ctx_row)
    return xl.reshape(bsz, n, d)
```

```python
import functools
import math

import jax
import jax.numpy as jnp
from jax import lax
from jax.experimental import pallas as pl
from jax.experimental.pallas import tpu as pltpu

F32 = jnp.float32
BF16 = jnp.bfloat16
HIGHEST = lax.Precision.HIGHEST

D_MODEL = 1024
DEPTH = 4
GRID_W = 64
HEAD_DIM = 64
FOURIER_W = 512
FOURIER_GROUP_W = 128
ATT_HEADS = 8
ATT_KV_HEADS = 2
ATT_GROUP = 4
Q_W = 512
KV_W = 128
ATT_BLOCK = 128
ROPE_BASE = 10000.0
SSM_W = 512
SSM_GROUP = 16
SSM_GROUPS = 32
SSM_STATE = 64
GMLP_W = 512
GMLP_GROUPS = 4
CHUNK = 128
N_EXPERTS = 8
ALPHA = (2 * DEPTH) ** 0.25
LN_EPS = 1e-5
NEG_INF = -1e30
LOG2E = math.log2(math.e)
QK_SCALE = LOG2E * HEAD_DIM ** -0.5

LANE = 128
MOD_ROWS = 24
VMEM_LIMIT = 56 * 1024 * 1024


def _params(sem, disable_bounds_checks=False):
    return pltpu.CompilerParams(dimension_semantics=sem, vmem_limit_bytes=VMEM_LIMIT,
                                disable_bounds_checks=disable_bounds_checks)


def _silu(x):
    return x * jax.nn.sigmoid(x)


def _layer_norm(z, g, b):
    mu = jnp.mean(z, axis=-1, keepdims=True)
    d = z - mu
    var = jnp.mean(d * d, axis=-1, keepdims=True)
    return d * lax.rsqrt(var + LN_EPS) * g + b


def _mod_spec(part, row_fn):
    return pl.BlockSpec((None, None, 1, D_MODEL), lambda i, *_: (part, row_fn(i), 0, 0))


def _row_spec(tm, width):
    return pl.BlockSpec((tm, width), lambda i, *_: (i, 0))


def _full_spec(shape):
    nd = len(shape)
    return pl.BlockSpec(shape, lambda *_: (0,) * nd)


def _mod_kernel(c_ref, w_ref, b_ref, o_ref):
    s = _silu(c_ref[...])
    o_ref[...] = jnp.dot(s, w_ref[...], preferred_element_type=F32, precision=HIGHEST) + b_ref[...]


def modulation(cvec, w_mod, b_mod):
    depth = w_mod.shape[0]
    out = pl.pallas_call(
        _mod_kernel,
        grid=(depth, 6),
        in_specs=[
            pl.BlockSpec((MOD_ROWS, D_MODEL), lambda l, j: (0, 0)),
            pl.BlockSpec((None, D_MODEL, D_MODEL), lambda l, j: (l, 0, j)),
            pl.BlockSpec((None, None, 1, D_MODEL), lambda l, j: (l, j, 0, 0)),
        ],
        out_specs=pl.BlockSpec((None, None, MOD_ROWS, D_MODEL), lambda l, j: (l, j, 0, 0)),
        out_shape=jax.ShapeDtypeStruct((depth, 6, MOD_ROWS, D_MODEL), F32),
        compiler_params=_params(("arbitrary", "arbitrary")),
        name="modulation",
    )(cvec, w_mod, b_mod.reshape(depth, 6, 1, D_MODEL))
    return out.reshape(depth, 6, MOD_ROWS, 1, D_MODEL)


def _rope_slab(x, cos, sin_lo, sin_hi):
    return (x * cos + pltpu.roll(x, LANE - 16, 1) * sin_lo + pltpu.roll(x, 16, 1) * sin_hi)


def _even_in_kernel(rope, x_ref, sc_ref, sh_ref, w_ref, *refs):
    if rope:
        cos_ref, slo_ref, shi_ref, a_ref, q_ref, k_ref, v_ref = refs
    else:
        a_ref, q_ref, k_ref, v_ref = refs
    h = (x_ref[...] * (1.0 + sc_ref[...]) + sh_ref[...]).astype(BF16)
    p = jnp.dot(h, w_ref[...], preferred_element_type=F32)
    a_ref[...] = p[:, :FOURIER_W].astype(BF16)
    v_ref[...] = p[:, FOURIER_W + Q_W + KV_W:].astype(BF16)
    n_qk = (Q_W + KV_W) // LANE
    for s in range(n_qk):
        slab = p[:, FOURIER_W + s * LANE:FOURIER_W + (s + 1) * LANE]
        if rope:
            slab = _rope_slab(slab, cos_ref[...], slo_ref[...], shi_ref[...])
        if s < Q_W // LANE:
            q_ref[:, s * LANE:(s + 1) * LANE] = (slab * QK_SCALE).astype(BF16)
        else:
            k_ref[...] = slab.astype(BF16)


def even_in_proj(x, mods, w_in, rope_tabs, tm, tiles_per_batch, row_fn):
    t = x.shape[0]
    rope = rope_tabs is not None
    in_specs = [_row_spec(tm, D_MODEL), _mod_spec(1, row_fn), _mod_spec(0, row_fn),
                _full_spec(w_in.shape)]
    args = [x, mods, mods, w_in]
    if rope:
        tab_spec = pl.BlockSpec((tm, LANE), lambda i: (i % tiles_per_batch, 0))
        in_specs += [tab_spec] * 3
        args += list(rope_tabs)
    return pl.pallas_call(
        functools.partial(_even_in_kernel, rope),
        grid=(t // tm,),
        in_specs=in_specs,
        out_specs=[_row_spec(tm, FOURIER_W), _row_spec(tm, Q_W), _row_spec(tm, KV_W),
                   _row_spec(tm, KV_W)],
        out_shape=[jax.ShapeDtypeStruct((t, FOURIER_W), BF16), jax.ShapeDtypeStruct((t, Q_W), BF16),
                   jax.ShapeDtypeStruct((t, KV_W), BF16), jax.ShapeDtypeStruct((t, KV_W), BF16)],
        compiler_params=_params(("parallel",)),
        name="even_in_proj",
    )(*args)


def _fourier_kernel(n, tr, scale, a_ref, cs_ref, w_ref, o_ref, r_ref):
    j = pl.program_id(1)

    @pl.when(j == 0)
    def _():
        rows = min(n, 512)

        def body(c, carry):
            r0 = pl.multiple_of(c * rows, rows)
            blk = a_ref[pl.ds(r0, rows), :]
            for g in range(FOURIER_W // FOURIER_GROUP_W):
                acs = jnp.dot(blk[:, g * LANE:(g + 1) * LANE], cs_ref[...],
                              preferred_element_type=F32)
                r_ref[pl.ds(r0, rows), g * LANE:(g + 1) * LANE] = acs[:, :LANE].astype(BF16)
                r_ref[pl.ds(n + r0, rows), g * LANE:(g + 1) * LANE] = acs[:, LANE:].astype(BF16)
            return carry

        lax.fori_loop(0, n // rows, body, 0)

    y = jnp.dot(w_ref[...], r_ref[...], preferred_element_type=F32)
    o_ref[...] = (y * scale).astype(BF16)


def fourier_mix(a, n, cs, w, tr):
    bsz = a.shape[0] // n
    nt = n // tr
    scale = 1.0 / math.sqrt(n * FOURIER_GROUP_W)
    return pl.pallas_call(
        functools.partial(_fourier_kernel, n, tr, scale),
        grid=(bsz, nt),
        in_specs=[pl.BlockSpec((n, FOURIER_W), lambda b, j: (b, 0)),
                  pl.BlockSpec(cs.shape, lambda b, j: (0, 0)),
                  pl.BlockSpec((tr, 2 * n), lambda b, j: (j, 0))],
        out_specs=pl.BlockSpec((tr, FOURIER_W), lambda b, j: (b * nt + j, 0)),
        out_shape=jax.ShapeDtypeStruct(a.shape, BF16),
        scratch_shapes=[pltpu.VMEM((2 * n, FOURIER_W), BF16)],
        compiler_params=_params(("parallel", "arbitrary")),
        name="fourier_mix",
    )(a, cs, w)


def _gqa_heads(q_ref, sink_ref, kv_of, valid, o_ref):
    nq = q_ref.shape[0]
    outs = []
    for h in range(ATT_KV_HEADS):
        kb, vb = kv_of(h)
        heads = [h * ATT_GROUP + g for g in range(ATT_GROUP)]
        qh = jnp.concatenate([q_ref[:, hd * HEAD_DIM:(hd + 1) * HEAD_DIM] for hd in heads], axis=0)
        sink = jnp.concatenate([jnp.full((nq, 1), sink_ref[hd], F32) for hd in heads], axis=0)
        s = lax.dot_general(qh, kb, (((1,), (1,)), ((), ())), preferred_element_type=F32)
        if valid is not None:
            v_prev, v_next = valid
            s = jnp.concatenate([jnp.where(v_prev, s[:, :ATT_BLOCK], NEG_INF),
                                 s[:, ATT_BLOCK:2 * ATT_BLOCK],
                                 jnp.where(v_next, s[:, 2 * ATT_BLOCK:3 * ATT_BLOCK], NEG_INF),
                                 s[:, 3 * ATT_BLOCK:]], axis=1)
        m = jnp.maximum(jnp.max(s, axis=-1, keepdims=True), sink)
        e = jnp.exp2(s - m)
        den = jnp.sum(e, axis=-1, keepdims=True) + jnp.exp2(sink - m)
        o = jnp.dot(e.astype(BF16), vb, preferred_element_type=F32) * (1.0 / den)
        outs += [o[g * nq:(g + 1) * nq] for g in range(ATT_GROUP)]
    o_ref[...] = jnp.concatenate(outs, axis=1).astype(BF16)


def _win_attn_kernel(n_blk, sink_ref, q_ref, kp_ref, kc_ref, kn_ref, vp_ref, vc_ref, vn_ref,
                     kx_ref, vx_ref, o_ref):
    blk = pl.program_id(1)
    shape = (ATT_GROUP * ATT_BLOCK, ATT_BLOCK)
    qi = lax.broadcasted_iota(jnp.int32, shape, 0) & (ATT_BLOCK - 1)
    kj = lax.broadcasted_iota(jnp.int32, shape, 1)
    k_min = jnp.where(blk == 0, ATT_BLOCK, 0)
    k_lim = jnp.where(blk == n_blk - 1, -1, ATT_BLOCK - 1)
    valid = (kj >= jnp.maximum(qi, k_min), kj <= jnp.minimum(qi, k_lim))

    def kv_of(h):
        cols = slice(h * HEAD_DIM, (h + 1) * HEAD_DIM)
        kb = jnp.concatenate([kp_ref[:, cols], kc_ref[:, cols], kn_ref[:, cols], kx_ref[:, cols]], axis=0)
        vb = jnp.concatenate([vp_ref[:, cols], vc_ref[:, cols], vn_ref[:, cols], vx_ref[:, cols]], axis=0)
        return kb, vb

    _gqa_heads(q_ref, sink_ref, kv_of, valid, o_ref)


def window_attention(q, k, v, k_ctx, v_ctx, sink, n, n_ctx):
    bsz = q.shape[0] // n
    n_blk = n // ATT_BLOCK

    def kv_spec(off):
        return pl.BlockSpec((ATT_BLOCK, KV_W),
                            lambda b, j: (b * n_blk + jnp.clip(j + off, 0, n_blk - 1), 0))

    ctx_spec = pl.BlockSpec((n_ctx, KV_W), lambda b, j: (b, 0))
    return pl.pallas_call(
        functools.partial(_win_attn_kernel, n_blk),
        grid=(bsz, n_blk),
        in_specs=[pl.BlockSpec(memory_space=pltpu.SMEM),
                  pl.BlockSpec((ATT_BLOCK, Q_W), lambda b, j: (b * n_blk + j, 0)),
                  kv_spec(-1), kv_spec(0), kv_spec(1), kv_spec(-1), kv_spec(0), kv_spec(1),
                  ctx_spec, ctx_spec],
        out_specs=pl.BlockSpec((ATT_BLOCK, Q_W), lambda b, j: (b * n_blk + j, 0)),
        out_shape=jax.ShapeDtypeStruct(q.shape, BF16),
        compiler_params=_params(("parallel", "arbitrary")),
        name="window_attention",
    )(sink, q, k, k, k, v, v, v, k_ctx, v_ctx)


def _ctx_attn_kernel(sink_ref, q_ref, k_ref, v_ref, o_ref):
    def kv_of(h):
        cols = slice(h * HEAD_DIM, (h + 1) * HEAD_DIM)
        return k_ref[:, cols], v_ref[:, cols]

    _gqa_heads(q_ref, sink_ref, kv_of, None, o_ref)


def context_attention(q, k, v, sink, n_ctx):
    bsz = q.shape[0] // n_ctx
    return pl.pallas_call(
        _ctx_attn_kernel,
        grid=(bsz,),
        in_specs=[pl.BlockSpec(memory_space=pltpu.SMEM), _row_spec(n_ctx, Q_W),
                  _row_spec(n_ctx, KV_W), _row_spec(n_ctx, KV_W)],
        out_specs=_row_spec(n_ctx, Q_W),
        out_shape=jax.ShapeDtypeStruct(q.shape, BF16),
        compiler_params=_params(("parallel",)),
        name="context_attention",
    )(sink, q, k, v)


def _even_out_kernel(f_ref, a_ref, x_ref, w_ref, g1_ref, sc_ref, sh_ref, lg_ref, lb_ref,
                     xo_ref, h_ref):
    half = w_ref.shape[0] // 2
    y = (jnp.dot(f_ref[...], w_ref[:half, :], preferred_element_type=F32)
         + jnp.dot(a_ref[...], w_ref[half:, :], preferred_element_type=F32))
    xn = _layer_norm(ALPHA * x_ref[...] + g1_ref[...] * y, lg_ref[...], lb_ref[...])
    xo_ref[...] = xn
    h_ref[...] = (xn * (1.0 + sc_ref[...]) + sh_ref[...]).astype(BF16)


def even_out_proj(f, a, x, w_out, mods, ln_g, ln_b, tm, row_fn):
    t = x.shape[0]
    vec = _full_spec((1, D_MODEL))
    return pl.pallas_call(
        _even_out_kernel,
        grid=(t // tm,),
        in_specs=[_row_spec(tm, FOURIER_W), _row_spec(tm, Q_W), _row_spec(tm, D_MODEL),
                  _full_spec(w_out.shape), _mod_spec(2, row_fn), _mod_spec(4, row_fn),
                  _mod_spec(3, row_fn), vec, vec],
        out_specs=[_row_spec(tm, D_MODEL), _row_spec(tm, D_MODEL)],
        out_shape=[jax.ShapeDtypeStruct((t, D_MODEL), F32), jax.ShapeDtypeStruct((t, D_MODEL), BF16)],
        compiler_params=_params(("parallel",)),
        name="even_out_proj",
    )(f, a, x, w_out, mods, mods, mods, ln_g, ln_b)


def _dense_ffn_kernel(h_ref, x_ref, w1_ref, w3_ref, w2_ref, g2_ref, lg_ref, lb_ref, o_ref, acc_ref):
    j = pl.program_id(1)

    @pl.when(j == 0)
    def _():
        acc_ref[...] = jnp.zeros_like(acc_ref)

    h = h_ref[...]
    he = _silu(jnp.dot(h, w1_ref[...], preferred_element_type=F32)) * jnp.dot(
        h, w3_ref[...], preferred_element_type=F32)
    acc_ref[...] += jnp.dot(he.astype(BF16), w2_ref[...], preferred_element_type=F32)

    @pl.when(j == pl.num_programs(1) - 1)
    def _():
        o_ref[...] = _layer_norm(ALPHA * x_ref[...] + g2_ref[...] * acc_ref[...], lg_ref[...], lb_ref[...])


def dense_ffn(h, x, w1, w3, w2, mods, ln_g, ln_b, tm, th, row_fn):
    t = x.shape[0]
    hidden = w1.shape[1]
    vec = pl.BlockSpec((1, D_MODEL), lambda i, j: (0, 0))
    return pl.pallas_call(
        _dense_ffn_kernel,
        grid=(t // tm, hidden // th),
        in_specs=[pl.BlockSpec((tm, D_MODEL), lambda i, j: (i, 0)),
                  pl.BlockSpec((tm, D_MODEL), lambda i, j: (i, 0)),
                  pl.BlockSpec((D_MODEL, th), lambda i, j: (0, j)),
                  pl.BlockSpec((D_MODEL, th), lambda i, j: (0, j)),
                  pl.BlockSpec((th, D_MODEL), lambda i, j: (j, 0)),
                  pl.BlockSpec((None, None, 1, D_MODEL), lambda i, j: (5, row_fn(i), 0, 0)),
                  vec, vec],
        out_specs=pl.BlockSpec((tm, D_MODEL), lambda i, j: (i, 0)),
        out_shape=jax.ShapeDtypeStruct((t, D_MODEL), F32),
        scratch_shapes=[pltpu.VMEM((tm, D_MODEL), F32)],
        compiler_params=_params(("parallel", "arbitrary")),
        name="dense_ffn",
    )(h, x, w1, w3, w2, mods, ln_g, ln_b)


def _odd_in_kernel(x_ref, sc_ref, sh_ref, wt_ref, wg_ref, ut_ref, gm_ref):
    h = (x_ref[...] * (1.0 + sc_ref[...]) + sh_ref[...]).astype(BF16)
    gm_ref[...] = jnp.dot(h, wg_ref[...], preferred_element_type=F32)
    for cc in range(ut_ref.shape[0]):
        hc = h[cc * CHUNK:(cc + 1) * CHUNK, :]
        ut = lax.dot_general(wt_ref[...], hc, (((1,), (1,)), ((), ())), preferred_element_type=F32)
        ut_ref[cc] = ut.astype(BF16)


def odd_in_proj(x, mods, w_ssm_t, w_gmlp, bsz, tm, row_fn):
    t = x.shape[0]
    n = t // bsz
    tiles_per_batch = n // tm
    cpt = tm // CHUNK
    return pl.pallas_call(
        _odd_in_kernel,
        grid=(t // tm,),
        in_specs=[_row_spec(tm, D_MODEL), _mod_spec(1, row_fn), _mod_spec(0, row_fn),
                  _full_spec(w_ssm_t.shape), _full_spec(w_gmlp.shape)],
        out_specs=[pl.BlockSpec((cpt, None, SSM_W, CHUNK),
                                lambda i: (i % tiles_per_batch, i // tiles_per_batch, 0, 0)),
                   _row_spec(tm, 2 * GMLP_W)],
        out_shape=[jax.ShapeDtypeStruct((n // CHUNK, bsz, SSM_W, CHUNK), BF16),
                   jax.ShapeDtypeStruct((t, 2 * GMLP_W), F32)],
        compiler_params=_params(("parallel",)),
        name="odd_in_proj",
    )(x, mods, mods, w_ssm_t, w_gmlp)


def _s5_kernel(n_cc, n_lc, bsz, uc_ref, ul_ref, kd_ref, g_ref, e_ref, a_ref, yc_ref, yl_ref, m_ref):
    n_ch = n_cc + n_lc
    for blk in range(SSM_GROUP * SSM_GROUP):
        s, tt = divmod(blk, SSM_GROUP)
        diag = jnp.broadcast_to(kd_ref[blk:blk + 1, :], (CHUNK, 2 * CHUNK))
        toe = pltpu.roll(diag, CHUNK + 1, 1, stride=1, stride_axis=0)
        m_ref[s * CHUNK:(s + 1) * CHUNK, tt * CHUNK:(tt + 1) * CHUNK] = toe[:, :CHUNK].astype(BF16)
    slabs = []
    for s in range(SSM_GROUP):
        sc = uc_ref[:, :, s, :].reshape(n_cc * bsz, CHUNK)
        sl = ul_ref[:, :, s, :].reshape(n_lc * bsz, CHUNK)
        slabs.append(jnp.concatenate([sc, sl], axis=0))
    lhs = jnp.concatenate(slabs, axis=1)
    y = jnp.dot(lhs, m_ref[...], preferred_element_type=F32)
    sm = jnp.dot(lhs, g_ref[...], preferred_element_type=F32)
    a = a_ref[...]
    n_st = 2 * SSM_STATE

    def step(state, a1, a2, inc):
        return a1 * state + a2 * pltpu.roll(state, SSM_STATE, 1) + inc

    state = jnp.zeros((bsz, n_st), F32)
    hf = []
    for k in range(n_ch):
        hf.append(state)
        state = step(state, a[0:1, :], a[1:2, :], sm[k * bsz:(k + 1) * bsz, :n_st])
    order = list(range(n_cc - 1, -1, -1)) + list(range(n_ch - 1, n_cc - 1, -1))
    state = jnp.zeros((bsz, n_st), F32)
    hr = [None] * n_ch
    for k in order:
        hr[k] = state
        state = step(state, a[2:3, :], a[3:4, :], sm[k * bsz:(k + 1) * bsz, n_st:])
    h_in = jnp.concatenate([jnp.concatenate(hf, axis=0), jnp.concatenate(hr, axis=0)], axis=1)
    y = y + jnp.dot(h_in.astype(BF16), e_ref[...], preferred_element_type=F32)
    rc = n_cc * bsz
    for s in range(SSM_GROUP):
        cols = slice(s * CHUNK, (s + 1) * CHUNK)
        yc_ref[:, :, s, :] = y[:rc, cols].reshape(n_cc, bsz, CHUNK)
        yl_ref[:, :, s, :] = y[rc:, cols].reshape(n_lc, bsz, CHUNK)


def s5_scan(ut_ctx, ut_lat, kd, g, e, a):
    n_cc, bsz = ut_ctx.shape[:2]
    n_lc = ut_lat.shape[0]

    def grp(nc):
        return pl.BlockSpec((nc, bsz, SSM_GROUP, CHUNK), lambda i: (0, 0, i, 0))

    def par(arr):
        return pl.BlockSpec((None,) + arr.shape[1:], lambda i: (i, 0, 0))

    return pl.pallas_call(
        functools.partial(_s5_kernel, n_cc, n_lc, bsz),
        grid=(SSM_GROUPS,),
        in_specs=[grp(n_cc), grp(n_lc), par(kd), par(g), par(e), par(a)],
        out_specs=[grp(n_cc), grp(n_lc)],
        out_shape=[jax.ShapeDtypeStruct(ut_ctx.shape, F32), jax.ShapeDtypeStruct(ut_lat.shape, F32)],
        scratch_shapes=[pltpu.VMEM((SSM_GROUP * CHUNK, SSM_GROUP * CHUNK), BF16)],
        compiler_params=_params(("parallel",)),
        name="s5_scan",
    )(ut_ctx, ut_lat, kd, g, e, a)


def s5_matrices(a_re, a_im, log_dt, b_re, b_im, c_re, c_im):
    t = CHUNK
    k_idx = jnp.arange(t + 1, dtype=F32)

    def one_dir(d):
        lam_re, lam_im = a_re[d].astype(F32), a_im[d].astype(F32)
        dt = jnp.exp(log_dt[d].astype(F32))[:, None]
        mag = jnp.exp(k_idx[:, None, None] * (lam_re * dt)[None])
        ang = k_idx[:, None, None] * (lam_im * dt)[None]
        pw_re, pw_im = mag * jnp.cos(ang), mag * jnp.sin(ang)
        nr, ni = pw_re[1] - 1.0, pw_im[1]
        den = lam_re * lam_re + lam_im * lam_im
        fr, fi = (nr * lam_re + ni * lam_im) / den, (ni * lam_re - nr * lam_im) / den
        br, bi = b_re[d].astype(F32), b_im[d].astype(F32)
        bb_re = fr[..., None] * br - fi[..., None] * bi
        bb_im = fr[..., None] * bi + fi[..., None] * br
        cr, ci = c_re[d].astype(F32), c_im[d].astype(F32)
        return pw_re, pw_im, bb_re, bb_im, cr, ci

    def conv_kernel(pw_re, pw_im, bb_re, bb_im, cr, ci):
        w_re = pw_re[:t, :, :, None] * bb_re[None] - pw_im[:t, :, :, None] * bb_im[None]
        w_im = pw_re[:t, :, :, None] * bb_im[None] + pw_im[:t, :, :, None] * bb_re[None]
        return (jnp.einsum('gtp,kgps->gkts', cr, w_re, precision=HIGHEST)
                - jnp.einsum('gtp,kgps->gkts', ci, w_im, precision=HIGHEST))

    fw, rv = one_dir(0), one_dir(1)
    kf, kr = conv_kernel(*fw), conv_kernel(*rv)
    kd = jnp.concatenate([kr[:, :0:-1], kf[:, :1] + kr[:, :1], kf[:, 1:],
                          jnp.zeros_like(kf[:, :1])], axis=1)
    kd = jnp.transpose(kd, (0, 3, 2, 1)).reshape(SSM_GROUPS, SSM_GROUP * SSM_GROUP, 2 * t)

    def summaries(pw_re, pw_im, bb_re, bb_im, cr, ci, reverse):
        pj_re = pw_re[:t] if reverse else pw_re[t - 1::-1][:t]
        pj_im = pw_im[:t] if reverse else pw_im[t - 1::-1][:t]
        g_re = jnp.einsum('jgp,gps->gsjp', pj_re, bb_re) - jnp.einsum('jgp,gps->gsjp', pj_im, bb_im)
        g_im = jnp.einsum('jgp,gps->gsjp', pj_re, bb_im) + jnp.einsum('jgp,gps->gsjp', pj_im, bb_re)
        pi_re = pw_re[t:0:-1] if reverse else pw_re[1:]
        pi_im = pw_im[t:0:-1] if reverse else pw_im[1:]
        e_re = jnp.einsum('gtp,igp->gpti', cr, pi_re) - jnp.einsum('gtp,igp->gpti', ci, pi_im)
        e_im = -(jnp.einsum('gtp,igp->gpti', cr, pi_im) + jnp.einsum('gtp,igp->gpti', ci, pi_re))
        a1 = jnp.concatenate([pw_re[t], pw_re[t]], axis=-1)
        a2 = jnp.concatenate([-pw_im[t], pw_im[t]], axis=-1)
        return g_re, g_im, e_re, e_im, a1, a2

    gf = summaries(*fw, False)
    gr = summaries(*rv, True)
    g = jnp.concatenate([gf[0], gf[1], gr[0], gr[1]], axis=-1)
    g = g.reshape(SSM_GROUPS, SSM_GROUP * t, 4 * SSM_STATE).astype(BF16)
    e = jnp.concatenate([gf[2], gf[3], gr[2], gr[3]], axis=1)
    e = e.reshape(SSM_GROUPS, 4 * SSM_STATE, SSM_GROUP * t).astype(BF16)
    a = jnp.stack([gf[4], gf[5], gr[4], gr[5]], axis=1)
    return kd, g, e, a


def _odd_out_kernel(yt_ref, ut_ref, gm_ref, x_ref, dsk_ref, wgt_ref, bgl_ref, ws_ref, bs_ref,
                    w_ref, wr_ref, cin_ref, g1_ref, sc_ref, sh_ref, lg_ref, lb_ref,
                    xo_ref, h_ref, ri_ref, cout_ref, mix_ref, cnt_ref):
    for cc in range(yt_ref.shape[0]):
        rows = slice(cc * CHUNK, (cc + 1) * CHUNK)
        t_t = dsk_ref[...] * ut_ref[cc].astype(F32) + yt_ref[cc]
        g_t = jax.nn.gelu(t_t)
        z_t = jnp.dot(wgt_ref[...], g_t.astype(BF16), preferred_element_type=F32) + bgl_ref[...]
        s_t = g_t * jax.nn.sigmoid(z_t)
        mix_ref[rows, :SSM_W] = s_t.T.astype(BF16)
        uv = jax.nn.gelu(gm_ref[rows, :])
        v = uv[:, GMLP_W:]
        mu = jnp.mean(v, axis=-1, keepdims=True)
        dv = v - mu
        var = jnp.mean(dv * dv, axis=-1, keepdims=True)
        v = (dv * lax.rsqrt(var + LN_EPS)).astype(BF16)
        for hh in range(GMLP_GROUPS):
            cols = slice(hh * LANE, (hh + 1) * LANE)
            vs = jnp.dot(ws_ref[hh], v[:, cols], preferred_element_type=F32) + bs_ref[hh]
            mix_ref[rows, SSM_W + hh * LANE:SSM_W + (hh + 1) * LANE] = (uv[:, cols] * vs).astype(BF16)
    y = jnp.dot(mix_ref[...], w_ref[...], preferred_element_type=F32)
    xn = _layer_norm(ALPHA * x_ref[...] + g1_ref[...] * y, lg_ref[...], lb_ref[...])
    xo_ref[...] = xn
    h2 = xn * (1.0 + sc_ref[...]) + sh_ref[...]
    h_ref[:, :D_MODEL] = h2
    h_hi = h2.astype(BF16)
    h_lo = (h2 - h_hi.astype(F32)).astype(BF16)
    logits = (jnp.dot(h_hi, wr_ref[0], preferred_element_type=F32)
              + jnp.dot(h_lo, wr_ref[0], preferred_element_type=F32)
              + jnp.dot(h_hi, wr_ref[1], preferred_element_type=F32))
    lane = lax.broadcasted_iota(jnp.int32, logits.shape, 1)
    lg = jnp.where(lane < N_EXPERTS, logits, -jnp.inf)
    m1 = jnp.max(lg, axis=-1, keepdims=True)
    i1 = jnp.min(jnp.where(lg == m1, lane, LANE), axis=-1, keepdims=True)
    lg2 = jnp.where(lane == i1, -jnp.inf, lg)
    m2 = jnp.max(lg2, axis=-1, keepdims=True)
    i2 = jnp.min(jnp.where(lg2 == m2, lane, LANE), axis=-1, keepdims=True)
    e2 = jnp.exp(m2 - m1)
    w1 = 1.0 / (1.0 + e2)
    w2 = e2 / (1.0 + e2)
    first_low = i1 < i2
    g_lo = jnp.where(first_low, w1, w2)
    g_hi = jnp.where(first_low, w2, w1)
    h_ref[:, D_MODEL:] = jnp.where(lane == 0, g_lo, jnp.where(lane == 1, g_hi, 0.0))
    pair = jnp.minimum(i1, i2) * N_EXPERTS + jnp.maximum(i1, i2)
    @pl.when(pl.program_id(0) == 0)
    def _():
        cnt_ref[...] = cin_ref[...]

    tm = logits.shape[0]
    onehot = jnp.where(lane == pair, 1.0, 0.0)
    before = jnp.where(lax.broadcasted_iota(jnp.int32, (tm, tm), 1)
                       < lax.broadcasted_iota(jnp.int32, (tm, tm), 0), 1.0, 0.0).astype(BF16)
    seen = jnp.dot(before, onehot.astype(BF16), preferred_element_type=F32) + cnt_ref[...]
    rank = jnp.sum(onehot * seen, axis=-1, keepdims=True).astype(jnp.int32)
    cnt_ref[...] += jnp.sum(onehot, axis=0, keepdims=True)
    cout_ref[...] = cnt_ref[...]
    ri_ref[...] = jnp.where(lane == 0, pair, jnp.where(lane == 1, rank, 0))


def odd_out_proj(yt, ut, gm, x, d_skip, w_glu_t, b_glu, w_s, b_s, w_out, w_router, pair_counts, mods,
                 ln_g, ln_b, bsz, tm, row_fn):
    t = x.shape[0]
    n = t // bsz
    tiles_per_batch = n // tm
    cpt = tm // CHUNK
    vec = _full_spec((1, D_MODEL))
    chunk_spec = pl.BlockSpec((cpt, None, SSM_W, CHUNK),
                              lambda i: (i % tiles_per_batch, i // tiles_per_batch, 0, 0))
    return pl.pallas_call(
        _odd_out_kernel,
        grid=(t // tm,),
        in_specs=[chunk_spec, chunk_spec, _row_spec(tm, 2 * GMLP_W), _row_spec(tm, D_MODEL),
                  _full_spec(d_skip.shape), _full_spec(w_glu_t.shape), _full_spec(b_glu.shape),
                  _full_spec(w_s.shape), _full_spec(b_s.shape), _full_spec(w_out.shape),
                  _full_spec(w_router.shape), _full_spec((1, LANE)), _mod_spec(2, row_fn),
                  _mod_spec(4, row_fn), _mod_spec(3, row_fn), vec, vec],
        out_specs=[_row_spec(tm, D_MODEL), _row_spec(tm, MOE_ROW_W), _row_spec(tm, LANE),
                   _full_spec((1, LANE))],
        out_shape=[jax.ShapeDtypeStruct((t, D_MODEL), F32), jax.ShapeDtypeStruct((t, MOE_ROW_W), F32),
                   jax.ShapeDtypeStruct((t, LANE), jnp.int32), jax.ShapeDtypeStruct((1, LANE), F32)],
        scratch_shapes=[pltpu.VMEM((tm, D_MODEL), BF16), pltpu.VMEM((1, LANE), F32)],
        compiler_params=_params(("arbitrary",)),
        name="odd_out_proj",
    )(yt, ut, gm, x, d_skip, w_glu_t, b_glu, w_s, b_s, w_out, w_router, pair_counts, mods, mods, mods,
      ln_g, ln_b)


N_PAIRS = N_EXPERTS * (N_EXPERTS - 1) // 2
MOE_ROW_W = D_MODEL + LANE
ROW_DMA_UNROLL = 8
ROW_WAIT_GROUP = 64


def _drain_rows(n_rows, src_row, dst_row, sem):
    def body(_, carry):
        for _ in range(ROW_WAIT_GROUP):
            pltpu.make_async_copy(src_row, dst_row, sem).wait()
        return carry

    lax.fori_loop(0, n_rows // ROW_WAIT_GROUP, body, 0)


def _dispatch_kernel(idx_ref, h_ref, xs_in_ref, xs_ref, sem):
    del xs_in_ref
    tm = h_ref.shape[0]

    def issue(r, carry):
        pltpu.make_async_copy(h_ref.at[pl.ds(r, 1), :],
                              xs_ref.at[pl.ds(idx_ref[0, 0, r], 1), :], sem).start()
        return carry

    lax.fori_loop(0, tm, issue, 0, unroll=ROW_DMA_UNROLL)
    _drain_rows(tm, h_ref.at[pl.ds(0, 1), :], xs_ref.at[pl.ds(0, 1), :], sem)


def moe_dispatch(h, dest, xs, tm):
    t, width = h.shape
    return pl.pallas_call(
        _dispatch_kernel,
        grid=(t // tm,),
        in_specs=[pl.BlockSpec((1, 1, tm), lambda i: (i, 0, 0), memory_space=pltpu.SMEM),
                  _row_spec(tm, width), pl.BlockSpec(memory_space=pl.ANY)],
        out_specs=pl.BlockSpec(memory_space=pl.ANY),
        out_shape=jax.ShapeDtypeStruct(xs.shape, xs.dtype),
        input_output_aliases={2: 0},
        scratch_shapes=[pltpu.SemaphoreType.DMA],
        compiler_params=_params(("arbitrary",), disable_bounds_checks=True),
        name="moe_dispatch",
    )(dest.reshape(t // tm, 1, tm), h, xs)


def _expert_ffn_kernel(te_ref, tv_ref, x_ref, w1_ref, w3_ref, w2_ref, o_ref):
    i = pl.program_id(0)
    k = pl.program_id(1)
    j = pl.program_id(2)

    @pl.when(jnp.logical_and(k == 0, j == 0))
    def _():
        o_ref[...] = jnp.zeros_like(o_ref)

    @pl.when(tv_ref[i] > 0)
    def _():
        h = x_ref[:, :D_MODEL].astype(BF16)
        gate = jnp.where(k == 0, x_ref[:, D_MODEL:D_MODEL + 1], x_ref[:, D_MODEL + 1:D_MODEL + 2])
        he = _silu(jnp.dot(h, w1_ref[...], preferred_element_type=F32)) * jnp.dot(
            h, w3_ref[...], preferred_element_type=F32)
        o_ref[...] += gate * jnp.dot(he.astype(BF16), w2_ref[...], preferred_element_type=F32)


def expert_ffn(xs, tile_experts, tile_valid, w1, w3, w2, tm, th):
    r = xs.shape[0]
    n_tiles = r // tm
    hidden = w1.shape[2]
    grid_spec = pltpu.PrefetchScalarGridSpec(
        num_scalar_prefetch=2,
        grid=(n_tiles, 2, hidden // th),
        in_specs=[pl.BlockSpec((tm, MOE_ROW_W), lambda i, k, j, te, tv: (i, 0)),
                  pl.BlockSpec((None, D_MODEL, th), lambda i, k, j, te, tv: (te[k * n_tiles + i], 0, j * tv[i])),
                  pl.BlockSpec((None, D_MODEL, th), lambda i, k, j, te, tv: (te[k * n_tiles + i], 0, j * tv[i])),
                  pl.BlockSpec((None, th, D_MODEL), lambda i, k, j, te, tv: (te[k * n_tiles + i], j * tv[i], 0))],
        out_specs=pl.BlockSpec((tm, D_MODEL), lambda i, k, j, te, tv: (i, 0)),
    )
    return pl.pallas_call(
        _expert_ffn_kernel,
        grid_spec=grid_spec,
        out_shape=jax.ShapeDtypeStruct((r, D_MODEL), F32),
        compiler_params=_params(("arbitrary", "arbitrary", "arbitrary")),
        name="expert_ffn",
    )(tile_experts, tile_valid, xs, w1, w3, w2)


def _combine_kernel(idx_ref, nxt_ref, ys_ref, x_ref, g2_ref, lg_ref, lb_ref, o_ref, buf_ref, sem):
    i = pl.program_id(0)
    tm = o_ref.shape[0]
    slot = i % 2

    def gather(rows_ref, dst_slot):
        def issue(r, carry):
            pltpu.make_async_copy(ys_ref.at[pl.ds(rows_ref[0, 0, r], 1), :],
                                  buf_ref.at[dst_slot, pl.ds(r, 1), :], sem.at[dst_slot]).start()
            return carry

        lax.fori_loop(0, tm, issue, 0, unroll=ROW_DMA_UNROLL)

    @pl.when(i == 0)
    def _():
        gather(idx_ref, 0)

    @pl.when(i + 1 < pl.num_programs(0))
    def _():
        gather(nxt_ref, 1 - slot)

    _drain_rows(tm, ys_ref.at[pl.ds(0, 1), :], buf_ref.at[slot, pl.ds(0, 1), :], sem.at[slot])
    o_ref[...] = _layer_norm(ALPHA * x_ref[...] + g2_ref[...] * buf_ref[slot], lg_ref[...], lb_ref[...])


def moe_combine(ys, dest, x, mods, ln_g, ln_b, tm, row_fn):
    t = x.shape[0]
    n_steps = t // tm
    vec = _full_spec((1, D_MODEL))
    rows = dest.reshape(n_steps, 1, tm)
    return pl.pallas_call(
        _combine_kernel,
        grid=(n_steps,),
        in_specs=[pl.BlockSpec((1, 1, tm), lambda i: (i, 0, 0), memory_space=pltpu.SMEM),
                  pl.BlockSpec((1, 1, tm), lambda i: (jnp.minimum(i + 1, n_steps - 1), 0, 0),
                               memory_space=pltpu.SMEM),
                  pl.BlockSpec(memory_space=pl.ANY), _row_spec(tm, D_MODEL),
                  _mod_spec(5, row_fn), vec, vec],
        out_specs=_row_spec(tm, D_MODEL),
        out_shape=jax.ShapeDtypeStruct((t, D_MODEL), F32),
        scratch_shapes=[pltpu.VMEM((2, tm, D_MODEL), F32), pltpu.SemaphoreType.DMA((2,))],
        compiler_params=_params(("arbitrary",), disable_bounds_checks=True),
        name="moe_combine",
    )(rows, rows, ys, x, mods, ln_g, ln_b)


def route_plan(pair_counts, n_tokens, tm):
    n_tiles = n_tokens // tm + N_PAIRS
    counts = pair_counts.astype(jnp.int32)
    tiles_per = (counts + tm - 1) // tm
    tile_end = jnp.cumsum(tiles_per)
    row_start = (tile_end - tiles_per) * tm
    tile_ids = jnp.arange(n_tiles)
    tile_valid = (tile_ids < tile_end[-1]).astype(jnp.int32)
    last = jnp.maximum(tile_end[-1] - 1, 0)
    tile_pair = jnp.searchsorted(tile_end, jnp.minimum(tile_ids, last), side='right').astype(jnp.int32)
    tile_pair = jnp.minimum(tile_pair, N_EXPERTS * N_EXPERTS - 1)
    hi = tile_pair % N_EXPERTS
    tile_experts = jnp.concatenate([jnp.where(tile_valid > 0, tile_pair // N_EXPERTS, hi), hi])
    return row_start, tile_experts, tile_valid, n_tiles


def pair_rows(route_i, row_start):
    pair, rank = route_i[:, 0], route_i[:, 1]
    onehot = pair[:, None] == jnp.arange(N_EXPERTS * N_EXPERTS)[None, :]
    return rank + jnp.sum(jnp.where(onehot, row_start[None, :], 0), axis=1)


def _rope_tables(n):
    pos = jnp.arange(n, dtype=jnp.int32)
    rows, cols = (pos // GRID_W).astype(F32), (pos % GRID_W).astype(F32)
    quarter = HEAD_DIM // 4
    d = jnp.arange(HEAD_DIM)
    inv = ROPE_BASE ** (-(d % quarter).astype(F32) / quarter)
    ang = jnp.where(d[None, :] < HEAD_DIM // 2, rows[:, None], cols[:, None]) * inv[None, :]
    cos, sin = jnp.cos(ang), jnp.sin(ang)
    low = (d % (2 * quarter) < quarter)[None, :]
    tabs = (cos, jnp.where(low, -sin, 0.0), jnp.where(low, 0.0, sin))
    return tuple(jnp.tile(t, (1, LANE // HEAD_DIM)) for t in tabs)


def _dft_tables(n):
    idx = jnp.arange(n, dtype=jnp.int32)

    def table(rows, period):
        ang = ((rows[:, None] * idx[None, :]) % period).astype(F32) * (2.0 * math.pi / period)
        return jnp.cos(ang), jnp.sin(ang)

    if n <= GRID_W or n % GRID_W:
        return table(idx, n)
    m = n // GRID_W
    ca, sa = table(jnp.arange(m, dtype=jnp.int32), m)
    cb, sb = table(jnp.arange(GRID_W, dtype=jnp.int32), n)
    cos = ca[:, None, :] * cb[None, :, :] - sa[:, None, :] * sb[None, :, :]
    sin = sa[:, None, :] * cb[None, :, :] + ca[:, None, :] * sb[None, :, :]
    return cos.reshape(n, n), sin.reshape(n, n)


def _tile(total, want):
    return want if total % want == 0 else total


def kernel(x, c, ctx, c_ctx, w_mod, b_mod, ln_g, ln_b, ev_w_in, ev_w_out, ev_sink, ev_w1, ev_w3, ev_w2,
           od_w_in, ssm_a_re, ssm_a_im, ssm_log_dt, ssm_b_re, ssm_b_im, ssm_c_re, ssm_c_im, ssm_d,
           ssm_w_glu, ssm_b_glu, gmlp_w_s, gmlp_b_s, od_w_out, moe_w_router, moe_w1, moe_w3, moe_w2):
    bsz, n, d = x.shape
    n_ctx = ctx.shape[1]
    depth = w_mod.shape[0]
    assert d == D_MODEL and bsz + 1 <= MOD_ROWS and n % CHUNK == 0 and n_ctx % CHUNK == 0
    xl = x.reshape(bsz * n, d)
    xc = ctx.reshape(bsz * n_ctx, d)
    cvec = jnp.zeros((MOD_ROWS, d), F32).at[:bsz].set(c).at[bsz].set(c_ctx)
    mods_all = modulation(cvec, w_mod, b_mod)

    tm_l = _tile(n, 1024)
    tm_f = _tile(n, 512)

    def lat_row(tm):
        return lambda i: i // (n // tm)

    def ctx_row(i):
        return bsz

    rope_tabs = _rope_tables(n)
    cc_c, sc_c = _dft_tables(FOURIER_GROUP_W)
    cs = jnp.concatenate([cc_c, sc_c], axis=1).astype(BF16)
    cn, sn = _dft_tables(n)
    w_n = jnp.concatenate([cn, -sn], axis=1).astype(BF16)
    cx, sx = _dft_tables(n_ctx)
    w_x = jnp.concatenate([cx, -sx], axis=1).astype(BF16)

    for layer in range(depth):
        need_ctx = layer < depth - 1
        li = layer // 2
        mods = mods_all[layer]
        lg1, lb1 = ln_g[layer, 0][None, :], ln_b[layer, 0][None, :]
        lg2, lb2 = ln_g[layer, 1][None, :], ln_b[layer, 1][None, :]
        if layer % 2 == 0:
            w_in = ev_w_in[li].astype(BF16)
            w_out = ev_w_out[li].astype(BF16)
            sink = ev_sink[li].astype(F32) * LOG2E
            a_l, q_l, k_l, v_l = even_in_proj(xl, mods, w_in, rope_tabs, tm_l, n // tm_l, lat_row(tm_l))
            a_c, q_c, k_c, v_c = even_in_proj(xc, mods, w_in, None, n_ctx, 1, ctx_row)
            f_l = fourier_mix(a_l, n, cs, w_n, tm_f)
            at_l = window_attention(q_l, k_l, v_l, k_c, v_c, sink, n, n_ctx)
            xl, h_l = even_out_proj(f_l, at_l, xl, w_out, mods, lg1, lb1, tm_l, lat_row(tm_l))
            w1, w3, w2 = ev_w1[li].astype(BF16), ev_w3[li].astype(BF16), ev_w2[li].astype(BF16)
            th = _tile(w1.shape[1], w1.shape[1] // 2)
            xl = dense_ffn(h_l, xl, w1, w3, w2, mods, lg2, lb2, tm_l, th, lat_row(tm_l))
            if need_ctx:
                f_c = fourier_mix(a_c, n_ctx, cs, w_x, n_ctx)
                at_c = context_attention(q_c, k_c, v_c, sink, n_ctx)
                xc, h_c = even_out_proj(f_c, at_c, xc, w_out, mods, lg1, lb1, n_ctx, ctx_row)
                xc = dense_ffn(h_c, xc, w1, w3, w2, mods, lg2, lb2, n_ctx, th, ctx_row)
        else:
            w_in = od_w_in[li]
            w_ssm_t = w_in[:, :SSM_W].T.astype(BF16)
            w_gmlp = w_in[:, SSM_W:].astype(BF16)
            ut_l, gm_l = odd_in_proj(xl, mods, w_ssm_t, w_gmlp, bsz, tm_l, lat_row(tm_l))
            ut_c, gm_c = odd_in_proj(xc, mods, w_ssm_t, w_gmlp, bsz, n_ctx, ctx_row)
            m, g, e, a = s5_matrices(ssm_a_re[li], ssm_a_im[li], ssm_log_dt[li], ssm_b_re[li],
                                     ssm_b_im[li], ssm_c_re[li], ssm_c_im[li])
            yt_c, yt_l = s5_scan(ut_c, ut_l, m, g, e, a)
            w_r = jnp.pad(moe_w_router[li].astype(F32), ((0, 0), (0, LANE - N_EXPERTS)))
            w_r_hi = w_r.astype(BF16)
            w_r_split = jnp.stack([w_r_hi, (w_r - w_r_hi.astype(F32)).astype(BF16)])
            post = (ssm_d[li].astype(F32)[:, None], ssm_w_glu[li].T.astype(BF16),
                    ssm_b_glu[li].astype(F32)[:, None], gmlp_w_s[li].astype(BF16),
                    gmlp_b_s[li].astype(F32)[:, :, None], od_w_out[li].astype(BF16), w_r_split)
            counts = jnp.zeros((1, LANE), F32)
            xl, h_l, ri_l, counts = odd_out_proj(yt_l, ut_l, gm_l, xl, *post, counts, mods, lg1, lb1, bsz,
                                                 tm_l, lat_row(tm_l))
            n_tok = xl.shape[0]
            if need_ctx:
                xc, h_c, ri_c, counts = odd_out_proj(yt_c, ut_c, gm_c, xc, *post, counts, mods, lg1, lb1,
                                                     bsz, n_ctx, ctx_row)
                n_tok += xc.shape[0]
            tm_e = _tile(n_tok, 512)
            tm_d = _tile(n, 512)
            row_start, tile_experts, tile_valid, n_tiles = route_plan(
                counts[0, :N_EXPERTS * N_EXPERTS], n_tok, tm_e)
            dest_l = pair_rows(ri_l, row_start)
            xs = moe_dispatch(h_l, dest_l, jnp.zeros((n_tiles * tm_e, MOE_ROW_W), F32), tm_d)
            if need_ctx:
                dest_c = pair_rows(ri_c, row_start)
                xs = moe_dispatch(h_c, dest_c, xs, _tile(n_ctx, 512))
            w1, w3, w2 = moe_w1[li].astype(BF16), moe_w3[li].astype(BF16), moe_w2[li].astype(BF16)
            th = _tile(w1.shape[2], w1.shape[2] // 2)
            ys = expert_ffn(xs, tile_experts, tile_valid, w1, w3, w2, tm_e, th)
            xl = moe_combine(ys, dest_l, xl, mods, lg2, lb2, tm_d, lat_row(tm_d))
            if need_ctx:
                xc = moe_combine(ys, dest_c, xc, mods, lg2, lb2, _tile(n_ctx, 512), ctx_row)
    return xl.reshape(bsz, n, d)
```

```python
import functools
import math

import jax
import jax.numpy as jnp
from jax import lax
from jax.experimental import pallas as pl
from jax.experimental.pallas import tpu as pltpu

F32 = jnp.float32
BF16 = jnp.bfloat16
HIGHEST = lax.Precision.HIGHEST

D_MODEL = 1024
DEPTH = 4
GRID_W = 64
HEAD_DIM = 64
FOURIER_W = 512
FOURIER_GROUP_W = 128
ATT_HEADS = 8
ATT_KV_HEADS = 2
ATT_GROUP = 4
Q_W = 512
KV_W = 128
ATT_BLOCK = 128
ROPE_BASE = 10000.0
SSM_W = 512
SSM_GROUP = 16
SSM_GROUPS = 32
SSM_STATE = 64
GMLP_W = 512
GMLP_GROUPS = 4
CHUNK = 128
N_EXPERTS = 8
ALPHA = (2 * DEPTH) ** 0.25
LN_EPS = 1e-5
NEG_INF = -1e30
LOG2E = math.log2(math.e)
QK_SCALE = LOG2E * HEAD_DIM ** -0.5

LANE = 128
MOD_ROWS = 24
VMEM_LIMIT = 56 * 1024 * 1024


def _params(sem, disable_bounds_checks=False):
    return pltpu.CompilerParams(dimension_semantics=sem, vmem_limit_bytes=VMEM_LIMIT,
                                disable_bounds_checks=disable_bounds_checks)


def _silu(x):
    return x * jax.nn.sigmoid(x)


def _layer_norm(z, g, b):
    mu = jnp.mean(z, axis=-1, keepdims=True)
    d = z - mu
    var = jnp.mean(d * d, axis=-1, keepdims=True)
    return d * lax.rsqrt(var + LN_EPS) * g + b


def _mod_spec(part, row_fn):
    return pl.BlockSpec((None, None, 1, D_MODEL), lambda i, *_: (part, row_fn(i), 0, 0))


def _row_spec(tm, width):
    return pl.BlockSpec((tm, width), lambda i, *_: (i, 0))


def _full_spec(shape):
    nd = len(shape)
    return pl.BlockSpec(shape, lambda *_: (0,) * nd)


def _mod_kernel(c_ref, w_ref, b_ref, o_ref):
    s = _silu(c_ref[...])
    o_ref[...] = jnp.dot(s, w_ref[...], preferred_element_type=F32, precision=HIGHEST) + b_ref[...]


def modulation(cvec, w_mod, b_mod):
    depth = w_mod.shape[0]
    out = pl.pallas_call(
        _mod_kernel,
        grid=(depth, 6),
        in_specs=[
            pl.BlockSpec((MOD_ROWS, D_MODEL), lambda l, j: (0, 0)),
            pl.BlockSpec((None, D_MODEL, D_MODEL), lambda l, j: (l, 0, j)),
            pl.BlockSpec((None, None, 1, D_MODEL), lambda l, j: (l, j, 0, 0)),
        ],
        out_specs=pl.BlockSpec((None, None, MOD_ROWS, D_MODEL), lambda l, j: (l, j, 0, 0)),
        out_shape=jax.ShapeDtypeStruct((depth, 6, MOD_ROWS, D_MODEL), F32),
        compiler_params=_params(("arbitrary", "arbitrary")),
        name="modulation",
    )(cvec, w_mod, b_mod.reshape(depth, 6, 1, D_MODEL))
    return out.reshape(depth, 6, MOD_ROWS, 1, D_MODEL)


def _rope_slab(x, cos, sin_lo, sin_hi):
    return (x * cos + pltpu.roll(x, LANE - 16, 1) * sin_lo + pltpu.roll(x, 16, 1) * sin_hi)


def _even_in_kernel(rope, x_ref, sc_ref, sh_ref, w_ref, *refs):
    if rope:
        cos_ref, slo_ref, shi_ref, a_ref, q_ref, k_ref, v_ref = refs
    else:
        a_ref, q_ref, k_ref, v_ref = refs
    h = (x_ref[...] * (1.0 + sc_ref[...]) + sh_ref[...]).astype(BF16)
    p = jnp.dot(h, w_ref[...], preferred_element_type=F32)
    a_ref[...] = p[:, :FOURIER_W].astype(BF16)
    v_ref[...] = p[:, FOURIER_W + Q_W + KV_W:].astype(BF16)
    n_qk = (Q_W + KV_W) // LANE
    for s in range(n_qk):
        slab = p[:, FOURIER_W + s * LANE:FOURIER_W + (s + 1) * LANE]
        if rope:
            slab = _rope_slab(slab, cos_ref[...], slo_ref[...], shi_ref[...])
        if s < Q_W // LANE:
            q_ref[:, s * LANE:(s + 1) * LANE] = (slab * QK_SCALE).astype(BF16)
        else:
            k_ref[...] = slab.astype(BF16)


def even_in_proj(x, mods, w_in, rope_tabs, tm, tiles_per_batch, row_fn):
    t = x.shape[0]
    rope = rope_tabs is not None
    in_specs = [_row_spec(tm, D_MODEL), _mod_spec(1, row_fn), _mod_spec(0, row_fn),
                _full_spec(w_in.shape)]
    args = [x, mods, mods, w_in]
    if rope:
        tab_spec = pl.BlockSpec((tm, LANE), lambda i: (i % tiles_per_batch, 0))
        in_specs += [tab_spec] * 3
        args += list(rope_tabs)
    return pl.pallas_call(
        functools.partial(_even_in_kernel, rope),
        grid=(t // tm,),
        in_specs=in_specs,
        out_specs=[_row_spec(tm, FOURIER_W), _row_spec(tm, Q_W), _row_spec(tm, KV_W),
                   _row_spec(tm, KV_W)],
        out_shape=[jax.ShapeDtypeStruct((t, FOURIER_W), BF16), jax.ShapeDtypeStruct((t, Q_W), BF16),
                   jax.ShapeDtypeStruct((t, KV_W), BF16), jax.ShapeDtypeStruct((t, KV_W), BF16)],
        compiler_params=_params(("parallel",)),
        name="even_in_proj",
    )(*args)


def _fourier_kernel(n, tr, scale, a_ref, cs_ref, w_ref, o_ref, r_ref):
    j = pl.program_id(1)

    @pl.when(j == 0)
    def _():
        rows = min(n, 512)

        def body(c, carry):
            r0 = pl.multiple_of(c * rows, rows)
            blk = a_ref[pl.ds(r0, rows), :]
            for g in range(FOURIER_W // FOURIER_GROUP_W):
                acs = jnp.dot(blk[:, g * LANE:(g + 1) * LANE], cs_ref[...],
                              preferred_element_type=F32)
                r_ref[pl.ds(r0, rows), g * LANE:(g + 1) * LANE] = acs[:, :LANE].astype(BF16)
                r_ref[pl.ds(n + r0, rows), g * LANE:(g + 1) * LANE] = acs[:, LANE:].astype(BF16)
            return carry

        lax.fori_loop(0, n // rows, body, 0)

    y = jnp.dot(w_ref[...], r_ref[...], preferred_element_type=F32)
    o_ref[...] = (y * scale).astype(BF16)


def fourier_mix(a, n, cs, w, tr):
    bsz = a.shape[0] // n
    nt = n // tr
    scale = 1.0 / math.sqrt(n * FOURIER_GROUP_W)
    return pl.pallas_call(
        functools.partial(_fourier_kernel, n, tr, scale),
        grid=(bsz, nt),
        in_specs=[pl.BlockSpec((n, FOURIER_W), lambda b, j: (b, 0)),
                  pl.BlockSpec(cs.shape, lambda b, j: (0, 0)),
                  pl.BlockSpec((tr, 2 * n), lambda b, j: (j, 0))],
        out_specs=pl.BlockSpec((tr, FOURIER_W), lambda b, j: (b * nt + j, 0)),
        out_shape=jax.ShapeDtypeStruct(a.shape, BF16),
        scratch_shapes=[pltpu.VMEM((2 * n, FOURIER_W), BF16)],
        compiler_params=_params(("parallel", "arbitrary")),
        name="fourier_mix",
    )(a, cs, w)


def _gqa_heads(q_ref, sink_ref, kv_of, valid, o_ref):
    nq = q_ref.shape[0]
    outs = []
    for h in range(ATT_KV_HEADS):
        kb, vb = kv_of(h)
        heads = [h * ATT_GROUP + g for g in range(ATT_GROUP)]
        qh = jnp.concatenate([q_ref[:, hd * HEAD_DIM:(hd + 1) * HEAD_DIM] for hd in heads], axis=0)
        sink = jnp.concatenate([jnp.full((nq, 1), sink_ref[hd], F32) for hd in heads], axis=0)
        s = lax.dot_general(qh, kb, (((1,), (1,)), ((), ())), preferred_element_type=F32)
        if valid is not None:
            v_prev, v_next = valid
            s = jnp.concatenate([jnp.where(v_prev, s[:, :ATT_BLOCK], NEG_INF),
                                 s[:, ATT_BLOCK:2 * ATT_BLOCK],
                                 jnp.where(v_next, s[:, 2 * ATT_BLOCK:3 * ATT_BLOCK], NEG_INF),
                                 s[:, 3 * ATT_BLOCK:]], axis=1)
        m = jnp.maximum(jnp.max(s, axis=-1, keepdims=True), sink)
        e = jnp.exp2(s - m)
        den = jnp.sum(e, axis=-1, keepdims=True) + jnp.exp2(sink - m)
        o = jnp.dot(e.astype(BF16), vb, preferred_element_type=F32) * (1.0 / den)
        outs += [o[g * nq:(g + 1) * nq] for g in range(ATT_GROUP)]
    o_ref[...] = jnp.concatenate(outs, axis=1).astype(BF16)


def _win_attn_kernel(n_blk, sink_ref, q_ref, kp_ref, kc_ref, kn_ref, vp_ref, vc_ref, vn_ref,
                     kx_ref, vx_ref, o_ref):
    blk = pl.program_id(1)
    shape = (ATT_GROUP * ATT_BLOCK, ATT_BLOCK)
    qi = lax.broadcasted_iota(jnp.int32, shape, 0) & (ATT_BLOCK - 1)
    kj = lax.broadcasted_iota(jnp.int32, shape, 1)
    k_min = jnp.where(blk == 0, ATT_BLOCK, 0)
    k_lim = jnp.where(blk == n_blk - 1, -1, ATT_BLOCK - 1)
    valid = (kj >= jnp.maximum(qi, k_min), kj <= jnp.minimum(qi, k_lim))

    def kv_of(h):
        cols = slice(h * HEAD_DIM, (h + 1) * HEAD_DIM)
        kb = jnp.concatenate([kp_ref[:, cols], kc_ref[:, cols], kn_ref[:, cols], kx_ref[:, cols]], axis=0)
        vb = jnp.concatenate([vp_ref[:, cols], vc_ref[:, cols], vn_ref[:, cols], vx_ref[:, cols]], axis=0)
        return kb, vb

    _gqa_heads(q_ref, sink_ref, kv_of, valid, o_ref)


def window_attention(q, k, v, k_ctx, v_ctx, sink, n, n_ctx):
    bsz = q.shape[0] // n
    n_blk = n // ATT_BLOCK

    def kv_spec(off):
        return pl.BlockSpec((ATT_BLOCK, KV_W),
                            lambda b, j: (b * n_blk + jnp.clip(j + off, 0, n_blk - 1), 0))

    ctx_spec = pl.BlockSpec((n_ctx, KV_W), lambda b, j: (b, 0))
    return pl.pallas_call(
        functools.partial(_win_attn_kernel, n_blk),
        grid=(bsz, n_blk),
        in_specs=[pl.BlockSpec(memory_space=pltpu.SMEM),
                  pl.BlockSpec((ATT_BLOCK, Q_W), lambda b, j: (b * n_blk + j, 0)),
                  kv_spec(-1), kv_spec(0), kv_spec(1), kv_spec(-1), kv_spec(0), kv_spec(1),
                  ctx_spec, ctx_spec],
        out_specs=pl.BlockSpec((ATT_BLOCK, Q_W), lambda b, j: (b * n_blk + j, 0)),
        out_shape=jax.ShapeDtypeStruct(q.shape, BF16),
        compiler_params=_params(("parallel", "arbitrary")),
        name="window_attention",
    )(sink, q, k, k, k, v, v, v, k_ctx, v_ctx)


def _ctx_attn_kernel(sink_ref, q_ref, k_ref, v_ref, o_ref):
    def kv_of(h):
        cols = slice(h * HEAD_DIM, (h + 1) * HEAD_DIM)
        return k_ref[:, cols], v_ref[:, cols]

    _gqa_heads(q_ref, sink_ref, kv_of, None, o_ref)


def context_attention(q, k, v, sink, n_ctx):
    bsz = q.shape[0] // n_ctx
    return pl.pallas_call(
        _ctx_attn_kernel,
        grid=(bsz,),
        in_specs=[pl.BlockSpec(memory_space=pltpu.SMEM), _row_spec(n_ctx, Q_W),
                  _row_spec(n_ctx, KV_W), _row_spec(n_ctx, KV_W)],
        out_specs=_row_spec(n_ctx, Q_W),
        out_shape=jax.ShapeDtypeStruct(q.shape, BF16),
        compiler_params=_params(("parallel",)),
        name="context_attention",
    )(sink, q, k, v)


def _even_out_kernel(f_ref, a_ref, x_ref, w_ref, g1_ref, sc_ref, sh_ref, lg_ref, lb_ref,
                     xo_ref, h_ref):
    half = w_ref.shape[0] // 2
    y = (jnp.dot(f_ref[...], w_ref[:half, :], preferred_element_type=F32)
         + jnp.dot(a_ref[...], w_ref[half:, :], preferred_element_type=F32))
    xn = _layer_norm(ALPHA * x_ref[...] + g1_ref[...] * y, lg_ref[...], lb_ref[...])
    xo_ref[...] = xn
    h_ref[...] = (xn * (1.0 + sc_ref[...]) + sh_ref[...]).astype(BF16)


def even_out_proj(f, a, x, w_out, mods, ln_g, ln_b, tm, row_fn):
    t = x.shape[0]
    vec = _full_spec((1, D_MODEL))
    return pl.pallas_call(
        _even_out_kernel,
        grid=(t // tm,),
        in_specs=[_row_spec(tm, FOURIER_W), _row_spec(tm, Q_W), _row_spec(tm, D_MODEL),
                  _full_spec(w_out.shape), _mod_spec(2, row_fn), _mod_spec(4, row_fn),
                  _mod_spec(3, row_fn), vec, vec],
        out_specs=[_row_spec(tm, D_MODEL), _row_spec(tm, D_MODEL)],
        out_shape=[jax.ShapeDtypeStruct((t, D_MODEL), F32), jax.ShapeDtypeStruct((t, D_MODEL), BF16)],
        compiler_params=_params(("parallel",)),
        name="even_out_proj",
    )(f, a, x, w_out, mods, mods, mods, ln_g, ln_b)


def _dense_ffn_kernel(h_ref, x_ref, w1_ref, w3_ref, w2_ref, g2_ref, lg_ref, lb_ref, o_ref, acc_ref):
    j = pl.program_id(1)

    @pl.when(j == 0)
    def _():
        acc_ref[...] = jnp.zeros_like(acc_ref)

    h = h_ref[...]
    he = _silu(jnp.dot(h, w1_ref[...], preferred_element_type=F32)) * jnp.dot(
        h, w3_ref[...], preferred_element_type=F32)
    acc_ref[...] += jnp.dot(he.astype(BF16), w2_ref[...], preferred_element_type=F32)

    @pl.when(j == pl.num_programs(1) - 1)
    def _():
        o_ref[...] = _layer_norm(ALPHA * x_ref[...] + g2_ref[...] * acc_ref[...], lg_ref[...], lb_ref[...])


def dense_ffn(h, x, w1, w3, w2, mods, ln_g, ln_b, tm, th, row_fn):
    t = x.shape[0]
    hidden = w1.shape[1]
    vec = pl.BlockSpec((1, D_MODEL), lambda i, j: (0, 0))
    return pl.pallas_call(
        _dense_ffn_kernel,
        grid=(t // tm, hidden // th),
        in_specs=[pl.BlockSpec((tm, D_MODEL), lambda i, j: (i, 0)),
                  pl.BlockSpec((tm, D_MODEL), lambda i, j: (i, 0)),
                  pl.BlockSpec((D_MODEL, th), lambda i, j: (0, j)),
                  pl.BlockSpec((D_MODEL, th), lambda i, j: (0, j)),
                  pl.BlockSpec((th, D_MODEL), lambda i, j: (j, 0)),
                  pl.BlockSpec((None, None, 1, D_MODEL), lambda i, j: (5, row_fn(i), 0, 0)),
                  vec, vec],
        out_specs=pl.BlockSpec((tm, D_MODEL), lambda i, j: (i, 0)),
        out_shape=jax.ShapeDtypeStruct((t, D_MODEL), F32),
        scratch_shapes=[pltpu.VMEM((tm, D_MODEL), F32)],
        compiler_params=_params(("parallel", "arbitrary")),
        name="dense_ffn",
    )(h, x, w1, w3, w2, mods, ln_g, ln_b)


def _odd_in_kernel(x_ref, sc_ref, sh_ref, wt_ref, wg_ref, ut_ref, gm_ref):
    h = (x_ref[...] * (1.0 + sc_ref[...]) + sh_ref[...]).astype(BF16)
    gm_ref[...] = jnp.dot(h, wg_ref[...], preferred_element_type=F32)
    for cc in range(ut_ref.shape[0]):
        hc = h[cc * CHUNK:(cc + 1) * CHUNK, :]
        ut = lax.dot_general(wt_ref[...], hc, (((1,), (1,)), ((), ())), preferred_element_type=F32)
        ut_ref[cc] = ut.astype(BF16)


def odd_in_proj(x, mods, w_ssm_t, w_gmlp, bsz, tm, row_fn):
    t = x.shape[0]
    n = t // bsz
    tiles_per_batch = n // tm
    cpt = tm // CHUNK
    return pl.pallas_call(
        _odd_in_kernel,
        grid=(t // tm,),
        in_specs=[_row_spec(tm, D_MODEL), _mod_spec(1, row_fn), _mod_spec(0, row_fn),
                  _full_spec(w_ssm_t.shape), _full_spec(w_gmlp.shape)],
        out_specs=[pl.BlockSpec((cpt, None, SSM_W, CHUNK),
                                lambda i: (i % tiles_per_batch, i // tiles_per_batch, 0, 0)),
                   _row_spec(tm, 2 * GMLP_W)],
        out_shape=[jax.ShapeDtypeStruct((n // CHUNK, bsz, SSM_W, CHUNK), BF16),
                   jax.ShapeDtypeStruct((t, 2 * GMLP_W), F32)],
        compiler_params=_params(("parallel",)),
        name="odd_in_proj",
    )(x, mods, mods, w_ssm_t, w_gmlp)


def _s5_kernel(n_cc, n_lc, bsz, uc_ref, ul_ref, kd_ref, g_ref, e_ref, a_ref, yc_ref, yl_ref, m_ref):
    n_ch = n_cc + n_lc
    for blk in range(SSM_GROUP * SSM_GROUP):
        s, tt = divmod(blk, SSM_GROUP)
        diag = jnp.broadcast_to(kd_ref[blk:blk + 1, :], (CHUNK, 2 * CHUNK))
        toe = pltpu.roll(diag, CHUNK + 1, 1, stride=1, stride_axis=0)
        m_ref[s * CHUNK:(s + 1) * CHUNK, tt * CHUNK:(tt + 1) * CHUNK] = toe[:, :CHUNK].astype(BF16)
    slabs = []
    for s in range(SSM_GROUP):
        sc = uc_ref[:, :, s, :].reshape(n_cc * bsz, CHUNK)
        sl = ul_ref[:, :, s, :].reshape(n_lc * bsz, CHUNK)
        slabs.append(jnp.concatenate([sc, sl], axis=0))
    lhs = jnp.concatenate(slabs, axis=1)
    y = jnp.dot(lhs, m_ref[...], preferred_element_type=F32)
    sm = jnp.dot(lhs, g_ref[...], preferred_element_type=F32)
    a = a_ref[...]
    n_st = 2 * SSM_STATE

    def step(state, a1, a2, inc):
        return a1 * state + a2 * pltpu.roll(state, SSM_STATE, 1) + inc

    state = jnp.zeros((bsz, n_st), F32)
    hf = []
    for k in range(n_ch):
        hf.append(state)
        state = step(state, a[0:1, :], a[1:2, :], sm[k * bsz:(k + 1) * bsz, :n_st])
    order = list(range(n_cc - 1, -1, -1)) + list(range(n_ch - 1, n_cc - 1, -1))
    state = jnp.zeros((bsz, n_st), F32)
    hr = [None] * n_ch
    for k in order:
        hr[k] = state
        state = step(state, a[2:3, :], a[3:4, :], sm[k * bsz:(k + 1) * bsz, n_st:])
    h_in = jnp.concatenate([jnp.concatenate(hf, axis=0), jnp.concatenate(hr, axis=0)], axis=1)
    y = y + jnp.dot(h_in.astype(BF16), e_ref[...], preferred_element_type=F32)
    rc = n_cc * bsz
    for s in range(SSM_GROUP):
        cols = slice(s * CHUNK, (s + 1) * CHUNK)
        yc_ref[:, :, s, :] = y[:rc, cols].reshape(n_cc, bsz, CHUNK)
        yl_ref[:, :, s, :] = y[rc:, cols].reshape(n_lc, bsz, CHUNK)


def s5_scan(ut_ctx, ut_lat, kd, g, e, a):
    n_cc, bsz = ut_ctx.shape[:2]
    n_lc = ut_lat.shape[0]

    def grp(nc):
        return pl.BlockSpec((nc, bsz, SSM_GROUP, CHUNK), lambda i: (0, 0, i, 0))

    def par(arr):
        return pl.BlockSpec((None,) + arr.shape[1:], lambda i: (i, 0, 0))

    return pl.pallas_call(
        functools.partial(_s5_kernel, n_cc, n_lc, bsz),
        grid=(SSM_GROUPS,),
        in_specs=[grp(n_cc), grp(n_lc), par(kd), par(g), par(e), par(a)],
        out_specs=[grp(n_cc), grp(n_lc)],
        out_shape=[jax.ShapeDtypeStruct(ut_ctx.shape, F32), jax.ShapeDtypeStruct(ut_lat.shape, F32)],
        scratch_shapes=[pltpu.VMEM((SSM_GROUP * CHUNK, SSM_GROUP * CHUNK), BF16)],
        compiler_params=_params(("parallel",)),
        name="s5_scan",
    )(ut_ctx, ut_lat, kd, g, e, a)


def s5_matrices(a_re, a_im, log_dt, b_re, b_im, c_re, c_im):
    t = CHUNK
    k_idx = jnp.arange(t + 1, dtype=F32)

    def one_dir(d):
        lam_re, lam_im = a_re[d].astype(F32), a_im[d].astype(F32)
        dt = jnp.exp(log_dt[d].astype(F32))[:, None]
        mag = jnp.exp(k_idx[:, None, None] * (lam_re * dt)[None])
        ang = k_idx[:, None, None] * (lam_im * dt)[None]
        pw_re, pw_im = mag * jnp.cos(ang), mag * jnp.sin(ang)
        nr, ni = pw_re[1] - 1.0, pw_im[1]
        den = lam_re * lam_re + lam_im * lam_im
        fr, fi = (nr * lam_re + ni * lam_im) / den, (ni * lam_re - nr * lam_im) / den
        br, bi = b_re[d].astype(F32), b_im[d].astype(F32)
        bb_re = fr[..., None] * br - fi[..., None] * bi
        bb_im = fr[..., None] * bi + fi[..., None] * br
        cr, ci = c_re[d].astype(F32), c_im[d].astype(F32)
        return pw_re, pw_im, bb_re, bb_im, cr, ci

    def conv_kernel(pw_re, pw_im, bb_re, bb_im, cr, ci):
        w_re = pw_re[:t, :, :, None] * bb_re[None] - pw_im[:t, :, :, None] * bb_im[None]
        w_im = pw_re[:t, :, :, None] * bb_im[None] + pw_im[:t, :, :, None] * bb_re[None]
        return (jnp.einsum('gtp,kgps->gkts', cr, w_re, precision=HIGHEST)
                - jnp.einsum('gtp,kgps->gkts', ci, w_im, precision=HIGHEST))

    fw, rv = one_dir(0), one_dir(1)
    kf, kr = conv_kernel(*fw), conv_kernel(*rv)
    kd = jnp.concatenate([kr[:, :0:-1], kf[:, :1] + kr[:, :1], kf[:, 1:],
                          jnp.zeros_like(kf[:, :1])], axis=1)
    kd = jnp.transpose(kd, (0, 3, 2, 1)).reshape(SSM_GROUPS, SSM_GROUP * SSM_GROUP, 2 * t)

    def summaries(pw_re, pw_im, bb_re, bb_im, cr, ci, reverse):
        pj_re = pw_re[:t] if reverse else pw_re[t - 1::-1][:t]
        pj_im = pw_im[:t] if reverse else pw_im[t - 1::-1][:t]
        g_re = jnp.einsum('jgp,gps->gsjp', pj_re, bb_re) - jnp.einsum('jgp,gps->gsjp', pj_im, bb_im)
        g_im = jnp.einsum('jgp,gps->gsjp', pj_re, bb_im) + jnp.einsum('jgp,gps->gsjp', pj_im, bb_re)
        pi_re = pw_re[t:0:-1] if reverse else pw_re[1:]
        pi_im = pw_im[t:0:-1] if reverse else pw_im[1:]
        e_re = jnp.einsum('gtp,igp->gpti', cr, pi_re) - jnp.einsum('gtp,igp->gpti', ci, pi_im)
        e_im = -(jnp.einsum('gtp,igp->gpti', cr, pi_im) + jnp.einsum('gtp,igp->gpti', ci, pi_re))
        a1 = jnp.concatenate([pw_re[t], pw_re[t]], axis=-1)
        a2 = jnp.concatenate([-pw_im[t], pw_im[t]], axis=-1)
        return g_re, g_im, e_re, e_im, a1, a2

    gf = summaries(*fw, False)
    gr = summaries(*rv, True)
    g = jnp.concatenate([gf[0], gf[1], gr[0], gr[1]], axis=-1)
    g = g.reshape(SSM_GROUPS, SSM_GROUP * t, 4 * SSM_STATE).astype(BF16)
    e = jnp.concatenate([gf[2], gf[3], gr[2], gr[3]], axis=1)
    e = e.reshape(SSM_GROUPS, 4 * SSM_STATE, SSM_GROUP * t).astype(BF16)
    a = jnp.stack([gf[4], gf[5], gr[4], gr[5]], axis=1)
    return kd, g, e, a


def _odd_out_kernel(yt_ref, ut_ref, gm_ref, x_ref, dsk_ref, wgt_ref, bgl_ref, ws_ref, bs_ref,
                    w_ref, wr_ref, cin_ref, g1_ref, sc_ref, sh_ref, lg_ref, lb_ref,
                    xo_ref, h_ref, ri_ref, cout_ref, mix_ref, cnt_ref):
    for cc in range(yt_ref.shape[0]):
        rows = slice(cc * CHUNK, (cc + 1) * CHUNK)
        t_t = dsk_ref[...] * ut_ref[cc].astype(F32) + yt_ref[cc]
        g_t = jax.nn.gelu(t_t)
        z_t = jnp.dot(wgt_ref[...], g_t.astype(BF16), preferred_element_type=F32) + bgl_ref[...]
        s_t = g_t * jax.nn.sigmoid(z_t)
        mix_ref[rows, :SSM_W] = s_t.T.astype(BF16)
        uv = jax.nn.gelu(gm_ref[rows, :])
        v = uv[:, GMLP_W:]
        mu = jnp.mean(v, axis=-1, keepdims=True)
        dv = v - mu
        var = jnp.mean(dv * dv, axis=-1, keepdims=True)
        v = (dv * lax.rsqrt(var + LN_EPS)).astype(BF16)
        for hh in range(GMLP_GROUPS):
            cols = slice(hh * LANE, (hh + 1) * LANE)
            vs = jnp.dot(ws_ref[hh], v[:, cols], preferred_element_type=F32) + bs_ref[hh]
            mix_ref[rows, SSM_W + hh * LANE:SSM_W + (hh + 1) * LANE] = (uv[:, cols] * vs).astype(BF16)
    y = jnp.dot(mix_ref[...], w_ref[...], preferred_element_type=F32)
    xn = _layer_norm(ALPHA * x_ref[...] + g1_ref[...] * y, lg_ref[...], lb_ref[...])
    xo_ref[...] = xn
    h2 = xn * (1.0 + sc_ref[...]) + sh_ref[...]
    h_ref[:, :D_MODEL] = h2
    h_hi = h2.astype(BF16)
    h_lo = (h2 - h_hi.astype(F32)).astype(BF16)
    logits = (jnp.dot(h_hi, wr_ref[0], preferred_element_type=F32)
              + jnp.dot(h_lo, wr_ref[0], preferred_element_type=F32)
              + jnp.dot(h_hi, wr_ref[1], preferred_element_type=F32))
    lane = lax.broadcasted_iota(jnp.int32, logits.shape, 1)
    lg = jnp.where(lane < N_EXPERTS, logits, -jnp.inf)
    m1 = jnp.max(lg, axis=-1, keepdims=True)
    i1 = jnp.min(jnp.where(lg == m1, lane, LANE), axis=-1, keepdims=True)
    lg2 = jnp.where(lane == i1, -jnp.inf, lg)
    m2 = jnp.max(lg2, axis=-1, keepdims=True)
    i2 = jnp.min(jnp.where(lg2 == m2, lane, LANE), axis=-1, keepdims=True)
    e2 = jnp.exp(m2 - m1)
    w1 = 1.0 / (1.0 + e2)
    w2 = e2 / (1.0 + e2)
    first_low = i1 < i2
    g_lo = jnp.where(first_low, w1, w2)
    g_hi = jnp.where(first_low, w2, w1)
    h_ref[:, D_MODEL:] = jnp.where(lane == 0, g_lo, jnp.where(lane == 1, g_hi, 0.0))
    pair = jnp.minimum(i1, i2) * N_EXPERTS + jnp.maximum(i1, i2)
    @pl.when(pl.program_id(0) == 0)
    def _():
        cnt_ref[...] = cin_ref[...]

    tm = logits.shape[0]
    onehot = jnp.where(lane == pair, 1.0, 0.0)
    before = jnp.where(lax.broadcasted_iota(jnp.int32, (tm, tm), 1)
                       < lax.broadcasted_iota(jnp.int32, (tm, tm), 0), 1.0, 0.0).astype(BF16)
    seen = jnp.dot(before, onehot.astype(BF16), preferred_element_type=F32) + cnt_ref[...]
    rank = jnp.sum(onehot * seen, axis=-1, keepdims=True).astype(jnp.int32)
    cnt_ref[...] += jnp.sum(onehot, axis=0, keepdims=True)
    cout_ref[...] = cnt_ref[...]
    ri_ref[...] = jnp.where(lane == 0, pair, jnp.where(lane == 1, rank, 0))


def odd_out_proj(yt, ut, gm, x, d_skip, w_glu_t, b_glu, w_s, b_s, w_out, w_router, pair_counts, mods,
                 ln_g, ln_b, bsz, tm, row_fn):
    t = x.shape[0]
    n = t // bsz
    tiles_per_batch = n // tm
    cpt = tm // CHUNK
    vec = _full_spec((1, D_MODEL))
    chunk_spec = pl.BlockSpec((cpt, None, SSM_W, CHUNK),
                              lambda i: (i % tiles_per_batch, i // tiles_per_batch, 0, 0))
    return pl.pallas_call(
        _odd_out_kernel,
        grid=(t // tm,),
        in_specs=[chunk_spec, chunk_spec, _row_spec(tm, 2 * GMLP_W), _row_spec(tm, D_MODEL),
                  _full_spec(d_skip.shape), _full_spec(w_glu_t.shape), _full_spec(b_glu.shape),
                  _full_spec(w_s.shape), _full_spec(b_s.shape), _full_spec(w_out.shape),
                  _full_spec(w_router.shape), _full_spec((1, LANE)), _mod_spec(2, row_fn),
                  _mod_spec(4, row_fn), _mod_spec(3, row_fn), vec, vec],
        out_specs=[_row_spec(tm, D_MODEL), _row_spec(tm, MOE_ROW_W), _row_spec(tm, LANE),
                   _full_spec((1, LANE))],
        out_shape=[jax.ShapeDtypeStruct((t, D_MODEL), F32), jax.ShapeDtypeStruct((t, MOE_ROW_W), F32),
                   jax.ShapeDtypeStruct((t, LANE), jnp.int32), jax.ShapeDtypeStruct((1, LANE), F32)],
        scratch_shapes=[pltpu.VMEM((tm, D_MODEL), BF16), pltpu.VMEM((1, LANE), F32)],
        compiler_params=_params(("arbitrary",)),
        name="odd_out_proj",
    )(yt, ut, gm, x, d_skip, w_glu_t, b_glu, w_s, b_s, w_out, w_router, pair_counts, mods, mods, mods,
      ln_g, ln_b)


N_PAIRS = N_EXPERTS * (N_EXPERTS - 1) // 2
MOE_ROW_W = D_MODEL + LANE
ROW_DMA_UNROLL = 8
ROW_WAIT_GROUP = 64


def _issue_rows(n_rows, start_row):
    def body(g, carry):
        for u in range(ROW_DMA_UNROLL):
            start_row(g * ROW_DMA_UNROLL + u, u % 2)
        return carry

    lax.fori_loop(0, n_rows // ROW_DMA_UNROLL, body, 0)


def _drain_rows(n_rows, src_row, dst_row, sem):
    def body(_, carry):
        for _ in range(ROW_WAIT_GROUP):
            pltpu.make_async_copy(src_row, dst_row, sem).wait()
        return carry

    lax.fori_loop(0, n_rows // ROW_WAIT_GROUP, body, 0)


def _dispatch_kernel(idx_ref, h_ref, xs_in_ref, xs_ref, sem):
    del xs_in_ref
    tm = h_ref.shape[0]

    def issue(r, priority):
        pltpu.make_async_copy(h_ref.at[pl.ds(r, 1), :],
                              xs_ref.at[pl.ds(idx_ref[0, 0, r], 1), :], sem).start(priority=priority)

    _issue_rows(tm, issue)
    _drain_rows(tm, h_ref.at[pl.ds(0, 1), :], xs_ref.at[pl.ds(0, 1), :], sem)


def moe_dispatch(h, dest, xs, tm):
    t, width = h.shape
    return pl.pallas_call(
        _dispatch_kernel,
        grid=(t // tm,),
        in_specs=[pl.BlockSpec((1, 1, tm), lambda i: (i, 0, 0), memory_space=pltpu.SMEM),
                  _row_spec(tm, width), pl.BlockSpec(memory_space=pl.ANY)],
        out_specs=pl.BlockSpec(memory_space=pl.ANY),
        out_shape=jax.ShapeDtypeStruct(xs.shape, xs.dtype),
        input_output_aliases={2: 0},
        scratch_shapes=[pltpu.SemaphoreType.DMA],
        compiler_params=_params(("arbitrary",), disable_bounds_checks=True),
        name="moe_dispatch",
    )(dest.reshape(t // tm, 1, tm), h, xs)


def _expert_ffn_kernel(te_ref, tv_ref, x_ref, w1_ref, w3_ref, w2_ref, o_ref):
    i = pl.program_id(0)
    k = pl.program_id(1)
    j = pl.program_id(2)

    @pl.when(jnp.logical_and(k == 0, j == 0))
    def _():
        o_ref[...] = jnp.zeros_like(o_ref)

    @pl.when(tv_ref[i] > 0)
    def _():
        h = x_ref[:, :D_MODEL].astype(BF16)
        gate = jnp.where(k == 0, x_ref[:, D_MODEL:D_MODEL + 1], x_ref[:, D_MODEL + 1:D_MODEL + 2])
        he = _silu(jnp.dot(h, w1_ref[...], preferred_element_type=F32)) * jnp.dot(
            h, w3_ref[...], preferred_element_type=F32)
        o_ref[...] += gate * jnp.dot(he.astype(BF16), w2_ref[...], preferred_element_type=F32)


def expert_ffn(xs, tile_experts, tile_valid, w1, w3, w2, tm, th):
    r = xs.shape[0]
    n_tiles = r // tm
    hidden = w1.shape[2]
    grid_spec = pltpu.PrefetchScalarGridSpec(
        num_scalar_prefetch=2,
        grid=(n_tiles, 2, hidden // th),
        in_specs=[pl.BlockSpec((tm, MOE_ROW_W), lambda i, k, j, te, tv: (i, 0)),
                  pl.BlockSpec((None, D_MODEL, th), lambda i, k, j, te, tv: (te[k * n_tiles + i], 0, j * tv[i])),
                  pl.BlockSpec((None, D_MODEL, th), lambda i, k, j, te, tv: (te[k * n_tiles + i], 0, j * tv[i])),
                  pl.BlockSpec((None, th, D_MODEL), lambda i, k, j, te, tv: (te[k * n_tiles + i], j * tv[i], 0))],
        out_specs=pl.BlockSpec((tm, D_MODEL), lambda i, k, j, te, tv: (i, 0)),
    )
    return pl.pallas_call(
        _expert_ffn_kernel,
        grid_spec=grid_spec,
        out_shape=jax.ShapeDtypeStruct((r, D_MODEL), F32),
        compiler_params=_params(("arbitrary", "arbitrary", "arbitrary")),
        name="expert_ffn",
    )(tile_experts, tile_valid, xs, w1, w3, w2)


def _combine_kernel(idx_ref, nxt_ref, ys_ref, x_ref, g2_ref, lg_ref, lb_ref, o_ref, buf_ref, sem):
    i = pl.program_id(0)
    tm = o_ref.shape[0]
    slot = i % 2

    def gather(rows_ref, dst_slot):
        def issue(r, priority):
            pltpu.make_async_copy(ys_ref.at[pl.ds(rows_ref[0, 0, r], 1), :],
                                  buf_ref.at[dst_slot, pl.ds(r, 1), :],
                                  sem.at[dst_slot]).start(priority=priority)

        _issue_rows(tm, issue)

    @pl.when(i == 0)
    def _():
        gather(idx_ref, 0)

    @pl.when(i + 1 < pl.num_programs(0))
    def _():
        gather(nxt_ref, 1 - slot)

    _drain_rows(tm, ys_ref.at[pl.ds(0, 1), :], buf_ref.at[slot, pl.ds(0, 1), :], sem.at[slot])
    o_ref[...] = _layer_norm(ALPHA * x_ref[...] + g2_ref[...] * buf_ref[slot], lg_ref[...], lb_ref[...])


def moe_combine(ys, dest, x, mods, ln_g, ln_b, tm, row_fn):
    t = x.shape[0]
    n_steps = t // tm
    vec = _full_spec((1, D_MODEL))
    rows = dest.reshape(n_steps, 1, tm)
    return pl.pallas_call(
        _combine_kernel,
        grid=(n_steps,),
        in_specs=[pl.BlockSpec((1, 1, tm), lambda i: (i, 0, 0), memory_space=pltpu.SMEM),
                  pl.BlockSpec((1, 1, tm), lambda i: (jnp.minimum(i + 1, n_steps - 1), 0, 0),
                               memory_space=pltpu.SMEM),
                  pl.BlockSpec(memory_space=pl.ANY), _row_spec(tm, D_MODEL),
                  _mod_spec(5, row_fn), vec, vec],
        out_specs=_row_spec(tm, D_MODEL),
        out_shape=jax.ShapeDtypeStruct((t, D_MODEL), F32),
        scratch_shapes=[pltpu.VMEM((2, tm, D_MODEL), F32), pltpu.SemaphoreType.DMA((2,))],
        compiler_params=_params(("arbitrary",), disable_bounds_checks=True),
        name="moe_combine",
    )(rows, rows, ys, x, mods, ln_g, ln_b)


def route_plan(pair_counts, n_tokens, tm):
    n_tiles = n_tokens // tm + N_PAIRS
    counts = pair_counts.astype(jnp.int32)
    tiles_per = (counts + tm - 1) // tm
    tile_end = jnp.cumsum(tiles_per)
    row_start = (tile_end - tiles_per) * tm
    tile_ids = jnp.arange(n_tiles)
    tile_valid = (tile_ids < tile_end[-1]).astype(jnp.int32)
    last = jnp.maximum(tile_end[-1] - 1, 0)
    tile_pair = jnp.searchsorted(tile_end, jnp.minimum(tile_ids, last), side='right').astype(jnp.int32)
    tile_pair = jnp.minimum(tile_pair, N_EXPERTS * N_EXPERTS - 1)
    hi = tile_pair % N_EXPERTS
    tile_experts = jnp.concatenate([jnp.where(tile_valid > 0, tile_pair // N_EXPERTS, hi), hi])
    return row_start, tile_experts, tile_valid, n_tiles


def pair_rows(route_i, row_start):
    pair, rank = route_i[:, 0], route_i[:, 1]
    onehot = pair[:, None] == jnp.arange(N_EXPERTS * N_EXPERTS)[None, :]
    return rank + jnp.sum(jnp.where(onehot, row_start[None, :], 0), axis=1)


def _rope_tables(n):
    pos = jnp.arange(n, dtype=jnp.int32)
    rows, cols = (pos // GRID_W).astype(F32), (pos % GRID_W).astype(F32)
    quarter = HEAD_DIM // 4
    d = jnp.arange(HEAD_DIM)
    inv = ROPE_BASE ** (-(d % quarter).astype(F32) / quarter)
    ang = jnp.where(d[None, :] < HEAD_DIM // 2, rows[:, None], cols[:, None]) * inv[None, :]
    cos, sin = jnp.cos(ang), jnp.sin(ang)
    low = (d % (2 * quarter) < quarter)[None, :]
    tabs = (cos, jnp.where(low, -sin, 0.0), jnp.where(low, 0.0, sin))
    return tuple(jnp.tile(t, (1, LANE // HEAD_DIM)) for t in tabs)


def _dft_tables(n):
    idx = jnp.arange(n, dtype=jnp.int32)

    def table(rows, period):
        ang = ((rows[:, None] * idx[None, :]) % period).astype(F32) * (2.0 * math.pi / period)
        return jnp.cos(ang), jnp.sin(ang)

    if n <= GRID_W or n % GRID_W:
        return table(idx, n)
    m = n // GRID_W
    ca, sa = table(jnp.arange(m, dtype=jnp.int32), m)
    cb, sb = table(jnp.arange(GRID_W, dtype=jnp.int32), n)
    cos = ca[:, None, :] * cb[None, :, :] - sa[:, None, :] * sb[None, :, :]
    sin = sa[:, None, :] * cb[None, :, :] + ca[:, None, :] * sb[None, :, :]
    return cos.reshape(n, n), sin.reshape(n, n)


def _tile(total, want):
    return want if total % want == 0 else total


def kernel(x, c, ctx, c_ctx, w_mod, b_mod, ln_g, ln_b, ev_w_in, ev_w_out, ev_sink, ev_w1, ev_w3, ev_w2,
           od_w_in, ssm_a_re, ssm_a_im, ssm_log_dt, ssm_b_re, ssm_b_im, ssm_c_re, ssm_c_im, ssm_d,
           ssm_w_glu, ssm_b_glu, gmlp_w_s, gmlp_b_s, od_w_out, moe_w_router, moe_w1, moe_w3, moe_w2):
    bsz, n, d = x.shape
    n_ctx = ctx.shape[1]
    depth = w_mod.shape[0]
    assert d == D_MODEL and bsz + 1 <= MOD_ROWS and n % CHUNK == 0 and n_ctx % CHUNK == 0
    xl = x.reshape(bsz * n, d)
    xc = ctx.reshape(bsz * n_ctx, d)
    cvec = jnp.zeros((MOD_ROWS, d), F32).at[:bsz].set(c).at[bsz].set(c_ctx)
    mods_all = modulation(cvec, w_mod, b_mod)

    tm_l = _tile(n, 1024)
    tm_f = _tile(n, 512)

    def lat_row(tm):
        return lambda i: i // (n // tm)

    def ctx_row(i):
        return bsz

    rope_tabs = _rope_tables(n)
    cc_c, sc_c = _dft_tables(FOURIER_GROUP_W)
    cs = jnp.concatenate([cc_c, sc_c], axis=1).astype(BF16)
    cn, sn = _dft_tables(n)
    w_n = jnp.concatenate([cn, -sn], axis=1).astype(BF16)
    cx, sx = _dft_tables(n_ctx)
    w_x = jnp.concatenate([cx, -sx], axis=1).astype(BF16)

    for layer in range(depth):
        need_ctx = layer < depth - 1
        li = layer // 2
        mods = mods_all[layer]
        lg1, lb1 = ln_g[layer, 0][None, :], ln_b[layer, 0][None, :]
        lg2, lb2 = ln_g[layer, 1][None, :], ln_b[layer, 1][None, :]
        if layer % 2 == 0:
            w_in = ev_w_in[li].astype(BF16)
            w_out = ev_w_out[li].astype(BF16)
            sink = ev_sink[li].astype(F32) * LOG2E
            a_l, q_l, k_l, v_l = even_in_proj(xl, mods, w_in, rope_tabs, tm_l, n // tm_l, lat_row(tm_l))
            a_c, q_c, k_c, v_c = even_in_proj(xc, mods, w_in, None, n_ctx, 1, ctx_row)
            f_l = fourier_mix(a_l, n, cs, w_n, tm_f)
            at_l = window_attention(q_l, k_l, v_l, k_c, v_c, sink, n, n_ctx)
            xl, h_l = even_out_proj(f_l, at_l, xl, w_out, mods, lg1, lb1, tm_l, lat_row(tm_l))
            w1, w3, w2 = ev_w1[li].astype(BF16), ev_w3[li].astype(BF16), ev_w2[li].astype(BF16)
            th = _tile(w1.shape[1], w1.shape[1] // 2)
            xl = dense_ffn(h_l, xl, w1, w3, w2, mods, lg2, lb2, tm_l, th, lat_row(tm_l))
            if need_ctx:
                f_c = fourier_mix(a_c, n_ctx, cs, w_x, n_ctx)
                at_c = context_attention(q_c, k_c, v_c, sink, n_ctx)
                xc, h_c = even_out_proj(f_c, at_c, xc, w_out, mods, lg1, lb1, n_ctx, ctx_row)
                xc = dense_ffn(h_c, xc, w1, w3, w2, mods, lg2, lb2, n_ctx, th, ctx_row)
        else:
            w_in = od_w_in[li]
            w_ssm_t = w_in[:, :SSM_W].T.astype(BF16)
            w_gmlp = w_in[:, SSM_W:].astype(BF16)
            ut_l, gm_l = odd_in_proj(xl, mods, w_ssm_t, w_gmlp, bsz, tm_l, lat_row(tm_l))
            ut_c, gm_c = odd_in_proj(xc, mods, w_ssm_t, w_gmlp, bsz, n_ctx, ctx_row)
            m, g, e, a = s5_matrices(ssm_a_re[li], ssm_a_im[li], ssm_log_dt[li], ssm_b_re[li],
                                     ssm_b_im[li], ssm_c_re[li], ssm_c_im[li])
            yt_c, yt_l = s5_scan(ut_c, ut_l, m, g, e, a)
            w_r = jnp.pad(moe_w_router[li].astype(F32), ((0, 0), (0, LANE - N_EXPERTS)))
            w_r_hi = w_r.astype(BF16)
            w_r_split = jnp.stack([w_r_hi, (w_r - w_r_hi.astype(F32)).astype(BF16)])
            post = (ssm_d[li].astype(F32)[:, None], ssm_w_glu[li].T.astype(BF16),
                    ssm_b_glu[li].astype(F32)[:, None], gmlp_w_s[li].astype(BF16),
                    gmlp_b_s[li].astype(F32)[:, :, None], od_w_out[li].astype(BF16), w_r_split)
            counts = jnp.zeros((1, LANE), F32)
            xl, h_l, ri_l, counts = odd_out_proj(yt_l, ut_l, gm_l, xl, *post, counts, mods, lg1, lb1, bsz,
                                                 tm_l, lat_row(tm_l))
            n_tok = xl.shape[0]
            if need_ctx:
                xc, h_c, ri_c, counts = odd_out_proj(yt_c, ut_c, gm_c, xc, *post, counts, mods, lg1, lb1,
                                                     bsz, n_ctx, ctx_row)
                n_tok += xc.shape[0]
            tm_e = _tile(n_tok, 512)
            tm_d = _tile(n, 512)
            row_start, tile_experts, tile_valid, n_tiles = route_plan(
                counts[0, :N_EXPERTS * N_EXPERTS], n_tok, tm_e)
            dest_l = pair_rows(ri_l, row_start)
            xs = moe_dispatch(h_l, dest_l, jnp.zeros((n_tiles * tm_e, MOE_ROW_W), F32), tm_d)
            if need_ctx:
                dest_c = pair_rows(ri_c, row_start)
                xs = moe_dispatch(h_c, dest_c, xs, _tile(n_ctx, 512))
            w1, w3, w2 = moe_w1[li].astype(BF16), moe_w3[li].astype(BF16), moe_w2[li].astype(BF16)
            th = _tile(w1.shape[2], w1.shape[2] // 2)
            ys = expert_ffn(xs, tile_experts, tile_valid, w1, w3, w2, tm_e, th)
            xl = moe_combine(ys, dest_l, xl, mods, lg2, lb2, tm_d, lat_row(tm_d))
            if need_ctx:
                xc = moe_combine(ys, dest_c, xc, mods, lg2, lb2, _tile(n_ctx, 512), ctx_row)
    return xl.reshape(bsz, n, d)
```
